```python
import jax, jax.numpy as jnp
from jax import lax
import numpy as np


D_MODEL = 1024
BATCH = 8
SEQ = 4096
DEPTH = 2

HEAD_DIM = 64
N_HEADS_A = D_MODEL // (2 * HEAD_DIM)
N_HEADS_B = D_MODEL // (2 * HEAD_DIM)
WIDTH_A = N_HEADS_A * HEAD_DIM
WIDTH_B = N_HEADS_B * HEAD_DIM
MIX_WIDTH = WIDTH_A + WIDTH_B
IN_COLS_A = 4 * WIDTH_A
IN_COLS = IN_COLS_A + 3 * WIDTH_B
D_DECAY_LORA = 64
D_AAA_LORA = 64
D_GATE_LORA = 128
DILATED_PAIRS = ((128, 1), (512, 4), (2048, 16))
ROPE_THETA = 10000.0
N_FOURIER_GROUPS = 8
FOURIER_GROUP = D_MODEL // N_FOURIER_GROUPS
D_FF = ((8 * D_MODEL // 3 + 127) // 128) * 128
N_EVEN = (DEPTH + 1) // 2
N_ODD = DEPTH // 2
RMS_EPS = 1e-6
LNX_EPS = 64e-5
L2_EPS = 1e-12
NEG_BIG = -1e30
F32 = jnp.float32

kernel_name = 'hybrid_rwkv7_dilated_fnet_block'


def rmsnorm(x, g):
    xf = x.astype(F32)
    y = xf * lax.rsqrt(jnp.mean(xf * xf, -1, keepdims=True) + RMS_EPS)
    return (y * g.astype(F32)).astype(x.dtype)


def dwconv3(x, w):
    xp = jnp.pad(x, ((0, 0), (1, 1), (0, 0)))
    return xp[:, :-2] * w[0] + xp[:, 1:-1] * w[1] + xp[:, 2:] * w[2]


def rope(x, pos):
    half = x.shape[-1] // 2
    inv = ROPE_THETA ** (-jnp.arange(half, dtype=F32) / half)
    ang = pos.astype(F32)[:, None] * inv[None, :]
    cos = jnp.cos(ang)[None, :, None, :]
    sin = jnp.sin(ang)[None, :, None, :]
    x1 = x[..., :half].astype(F32)
    x2 = x[..., half:].astype(F32)
    return jnp.concatenate([x1 * cos - x2 * sin, x2 * cos + x1 * sin], -1).astype(x.dtype)


def wkv7_scan(r, w, k, v, kk, a, reverse):
    B_, S_, H_, N_ = r.shape
    xs = tuple(jnp.moveaxis(t, 1, 0) for t in (r, w, k, v, kk, a))

    def step(state, inp):
        r_t, w_t, k_t, v_t, kk_t, a_t = inp
        sk = jnp.einsum('bhij,bhj->bhi', state, kk_t)
        state = (state * w_t[:, :, None, :]
                 - sk[..., :, None] * (kk_t * a_t)[:, :, None, :]
                 + v_t[..., :, None] * k_t[:, :, None, :])
        y = jnp.einsum('bhij,bhj->bhi', state, r_t)
        return state, y

    s0 = jnp.zeros((B_, H_, N_, N_), F32)
    _, ys = lax.scan(step, s0, xs, reverse=reverse)
    return jnp.moveaxis(ys, 0, 1)


def rwkv7_mixer(r, k, v, u, w0, w1, w2, a0, a1, a2, g1, g2, k_k, k_a, r_k, lnx_w, lnx_b):
    B_, S_, _ = r.shape
    rf, kf, vf, uf = (t.astype(F32) for t in (r, k, v, u))
    hs = lambda t: t.reshape(B_, S_, N_HEADS_A, HEAD_DIM)
    g = jax.nn.sigmoid(uf @ g1) @ g2
    kk = hs(kf * k_k)
    kk = kk * lax.rsqrt(jnp.sum(kk * kk, -1, keepdims=True) + L2_EPS)
    rh, vh = hs(rf), hs(vf)
    y = 0.0
    bonus = 0.0
    for d in range(2):
        w_log = -jax.nn.softplus(-(w0[d] + jnp.tanh(uf @ w1[d]) @ w2[d])) - 0.5
        decay = jnp.exp(-jnp.exp(w_log))
        a = jax.nn.sigmoid(a0[d] + (uf @ a1[d]) @ a2[d])
        kd = hs(kf * (1.0 + (a - 1.0) * k_a))
        ah = hs(a)
        y = y + wkv7_scan(rh, hs(decay), kd, vh, kk, ah, reverse=(d == 1))
        bonus = bonus + jnp.sum(rh * kd * r_k, -1, keepdims=True) * vh
    mu = jnp.mean(y, -1, keepdims=True)
    var = jnp.mean(jnp.square(y - mu), -1, keepdims=True)
    y = ((y - mu) * lax.rsqrt(var + LNX_EPS)).reshape(B_, S_, WIDTH_A) * lnx_w + lnx_b
    y = (y + bonus.reshape(B_, S_, WIDTH_A)) * g
    return y.astype(r.dtype)


def dilated_branch(q, k, v, window, dilation):
    B_, S_, H_, E_ = q.shape
    hk = window // (2 * dilation)
    M = S_ // dilation
    nb = -(-M // hk)
    Mp = nb * hk

    def blocks(t):
        t = t.reshape(B_, M, dilation, H_, E_)
        t = jnp.pad(t, ((0, 0), (0, Mp - M), (0, 0), (0, 0), (0, 0)))
        return t.reshape(B_, nb, hk, dilation, H_, E_)

    def band(t):
        tp = jnp.pad(t, ((0, 0), (1, 1), (0, 0), (0, 0), (0, 0), (0, 0)))
        return jnp.concatenate([tp[:, :-2], tp[:, 1:-1], tp[:, 2:]], axis=2)

    qb = blocks(q)
    kw = band(blocks(k))
    vw = band(blocks(v))
    s = jnp.einsum('bnqrhe,bnkrhe->bnrhqk', qb, kw).astype(F32) * (E_ ** -0.5)
    blk = jnp.arange(nb)[:, None, None]
    qi = blk * hk + jnp.arange(hk)[None, :, None]
    ki = (blk - 1) * hk + jnp.arange(3 * hk)[None, None, :]
    valid = (jnp.abs(ki - qi) <= hk) & (ki >= 0) & (ki < M)
    s = jnp.where(valid[None, :, None, None], s, NEG_BIG)
    m = jnp.max(s, -1)
    p = jnp.exp(s - m[..., None])
    den = jnp.sum(p, -1)
    num = jnp.einsum('bnrhqk,bnkrhe->bnqrhe', p.astype(v.dtype), vw).astype(F32)

    def unblock(t):
        t = t.reshape((B_, Mp, dilation) + t.shape[4:])[:, :M]
        return t.reshape((B_, S_) + t.shape[3:])

    m = unblock(jnp.moveaxis(m, -1, 2))
    den = unblock(jnp.moveaxis(den, -1, 2))
    num = unblock(num)
    return m, den, num


def dilated_attention(q, k, v):
    outs = [dilated_branch(q, k, v, w, d) for (w, d) in DILATED_PAIRS]
    m_all = jnp.max(jnp.stack([o[0] for o in outs], 0), 0)
    den = 0.0
    num = 0.0
    for m, dn, nm in outs:
        e = jnp.exp(m - m_all)
        den = den + e * dn
        num = num + e[..., None] * nm
    return num / den[..., None]


def parallel_mixer(h, mix_in, mix_conv, w0, w1, w2, a0, a1, a2, g1, g2,
                   k_k, k_a, r_k, lnx_w, lnx_b, mix_out):
    B_, S_, _ = h.shape
    z = h @ mix_in
    za = dwconv3(z[..., :IN_COLS_A], mix_conv)
    r, k, v, u = jnp.split(za, 4, -1)
    ya = rwkv7_mixer(r, k, v, u, w0, w1, w2, a0, a1, a2, g1, g2, k_k, k_a, r_k, lnx_w, lnx_b)
    qb, kb, vb = (t.reshape(B_, S_, N_HEADS_B, HEAD_DIM) for t in jnp.split(z[..., IN_COLS_A:], 3, -1))
    pos = jnp.arange(S_)
    qb = rope(qb, pos)
    kb = rope(kb, pos)
    yb = dilated_attention(qb, kb, vb).astype(h.dtype).reshape(B_, S_, WIDTH_B)
    return jnp.concatenate([ya, yb], -1) @ mix_out


def fourier_mixer(h, fnet_w):
    B_, S_, D_ = h.shape
    hg = h.astype(F32).reshape(B_, S_, N_FOURIER_GROUPS, FOURIER_GROUP)
    f = jnp.fft.fft2(hg, axes=(1, 3), norm='ortho').real
    return f.reshape(B_, S_, D_).astype(h.dtype) @ fnet_w


def conv_glu(h, up, conv, down):
    z = dwconv3(h @ up, conv)
    gate, val = jnp.split(z, 2, -1)
    return (jax.nn.gelu(gate, approximate=True) * val) @ down


def setup_inputs(seed: int = 0) -> dict:
    key = jax.random.key(seed)
    ks = jax.random.split(key, 26)
    nrm = lambda k, shape, s: jax.random.normal(k, shape, F32) * s
    ident3 = lambda k, C, n: jnp.broadcast_to(jnp.array([0.0, 1.0, 0.0], F32)[None, :, None], (n, 3, C)) + nrm(k, (n, 3, C), 0.2)
    return {
        'x': nrm(ks[0], (BATCH, SEQ, D_MODEL), 1.0),
        'c': nrm(ks[1], (BATCH, D_MODEL), 1.0),
        'ada_w': nrm(ks[2], (DEPTH, D_MODEL, 6 * D_MODEL), D_MODEL ** -0.5),
        'ada_b': nrm(ks[3], (DEPTH, 6 * D_MODEL), 0.02),
        'norm_g': 1.0 + nrm(ks[4], (DEPTH, 4, D_MODEL), 0.05),
        'mix_in': nrm(ks[5], (N_EVEN, D_MODEL, IN_COLS), D_MODEL ** -0.5),
        'mix_conv': ident3(ks[6], IN_COLS_A, N_EVEN),
        'rwkv_w0': jax.random.uniform(ks[7], (N_EVEN, 2, WIDTH_A), F32, -6.0, -1.0),
        'rwkv_w1': nrm(ks[8], (N_EVEN, 2, WIDTH_A, D_DECAY_LORA), WIDTH_A ** -0.5),
        'rwkv_w2': nrm(ks[9], (N_EVEN, 2, D_DECAY_LORA, WIDTH_A), 0.1 * D_DECAY_LORA ** -0.5),
        'rwkv_a0': nrm(ks[10], (N_EVEN, 2, WIDTH_A), 0.5),
        'rwkv_a1': nrm(ks[11], (N_EVEN, 2, WIDTH_A, D_AAA_LORA), WIDTH_A ** -0.5),
        'rwkv_a2': nrm(ks[12], (N_EVEN, 2, D_AAA_LORA, WIDTH_A), 0.5 * D_AAA_LORA ** -0.5),
        'rwkv_g1': nrm(ks[13], (N_EVEN, WIDTH_A, D_GATE_LORA), WIDTH_A ** -0.5),
        'rwkv_g2': nrm(ks[14], (N_EVEN, D_GATE_LORA, WIDTH_A), D_GATE_LORA ** -0.5),
        'rwkv_k_k': 0.85 + nrm(ks[15], (N_EVEN, WIDTH_A), 0.1),
        'rwkv_k_a': 1.0 + nrm(ks[16], (N_EVEN, WIDTH_A), 0.1),
        'rwkv_r_k': nrm(ks[17], (N_EVEN, N_HEADS_A, HEAD_DIM), 0.1),
        'rwkv_lnx_w': 1.0 + nrm(ks[18], (N_EVEN, WIDTH_A), 0.05),
        'rwkv_lnx_b': nrm(ks[19], (N_EVEN, WIDTH_A), 0.02),
        'mix_out': nrm(ks[20], (N_EVEN, MIX_WIDTH, D_MODEL), MIX_WIDTH ** -0.5),
        'fnet_w': nrm(ks[21], (N_ODD, D_MODEL, D_MODEL), D_MODEL ** -0.5),
        'ffn_up': nrm(ks[22], (DEPTH, D_MODEL, 2 * D_FF), D_MODEL ** -0.5),
        'ffn_conv': ident3(ks[23], 2 * D_FF, DEPTH),
        'ffn_down': nrm(ks[24], (DEPTH, D_FF, D_MODEL), D_FF ** -0.5),
    }


def reference(x, c, ada_w, ada_b, norm_g, mix_in, mix_conv, rwkv_w0, rwkv_w1, rwkv_w2,
              rwkv_a0, rwkv_a1, rwkv_a2, rwkv_g1, rwkv_g2, rwkv_k_k, rwkv_k_a, rwkv_r_k,
              rwkv_lnx_w, rwkv_lnx_b, mix_out, fnet_w, ffn_up, ffn_conv, ffn_down):
    cs = jax.nn.silu(c)
    for l in range(DEPTH):
        mod = cs @ ada_w[l] + ada_b[l]
        sh1, sc1, gt1, sh2, sc2, gt2 = [m[:, None, :] for m in jnp.split(mod, 6, -1)]
        h = rmsnorm(x, norm_g[l, 0]) * (1.0 + sc1) + sh1
        if l % 2 == 0:
            e = l // 2
            y = parallel_mixer(h, mix_in[e], mix_conv[e], rwkv_w0[e], rwkv_w1[e], rwkv_w2[e],
                               rwkv_a0[e], rwkv_a1[e], rwkv_a2[e], rwkv_g1[e], rwkv_g2[e],
                               rwkv_k_k[e], rwkv_k_a[e], rwkv_r_k[e], rwkv_lnx_w[e], rwkv_lnx_b[e],
                               mix_out[e])
        else:
            y = fourier_mixer(h, fnet_w[l // 2])
        x = x + gt1 * rmsnorm(y, norm_g[l, 1])
        h = rmsnorm(x, norm_g[l, 2]) * (1.0 + sc2) + sh2
        y = conv_glu(h, ffn_up[l], ffn_conv[l], ffn_down[l])
        x = x + gt2 * rmsnorm(y, norm_g[l, 3])
    return x
```

```python
import functools

import jax
import jax.numpy as jnp
import numpy as np
from jax import lax
from jax.experimental import pallas as pl
from jax.experimental.pallas import tpu as pltpu

F32 = jnp.float32
BF16 = jnp.bfloat16

HEAD_DIM = 64
DILATED_PAIRS = ((128, 1), (512, 4), (2048, 16))
ROPE_THETA = 10000.0
N_FOURIER_GROUPS = 8
RMS_EPS = 1e-6
LNX_EPS = 64e-5
L2_EPS = 1e-12
NEG_BIG = -1e30

LANES = 128
SUBLANES = 8
VMEM_LIMIT_BYTES = 56 * 1024 * 1024

PAIR = 2 * HEAD_DIM
assert PAIR == LANES
HALO = SUBLANES
CHUNK = 64
QBLK = 128
KBLK = 2 * QBLK


def _cparams(sem):
    return pltpu.CompilerParams(dimension_semantics=sem, vmem_limit_bytes=VMEM_LIMIT_BYTES)


def _resident(shape):
    zeros = (0,) * len(shape)
    return pl.BlockSpec(shape, lambda *_: zeros, pipeline_mode=pl.Buffered(1))


def _mm(a, b):
    return jnp.dot(a.astype(BF16), b.astype(BF16), preferred_element_type=F32)


def _mm_nt(a, b):
    return lax.dot_general(a.astype(BF16), b.astype(BF16), (((1,), (1,)), ((), ())),
                           preferred_element_type=F32)


def _mm_tn(a, b):
    return lax.dot_general(a.astype(BF16), b.astype(BF16), (((0,), (0,)), ((), ())),
                           preferred_element_type=F32)


def _split(x, n):
    parts = []
    rem = x
    for _ in range(n):
        p = rem.astype(BF16)
        parts.append(p)
        rem = rem - p.astype(F32)
    return parts


def _rmsnorm(x, g):
    return x * lax.rsqrt(jnp.mean(x * x, -1, keepdims=True) + RMS_EPS) * g


def _conv3_rows(z, w, n_rows):
    total = z.shape[0]
    y = pltpu.roll(z, 1, axis=0) * w[0:1] + z * w[1:2] + pltpu.roll(z, total - 1, axis=0) * w[2:3]
    return y[HALO:HALO + n_rows]


def _halo_specs(tm, width, seq, col_block=0):
    per = tm // HALO
    last = seq // HALO - 1

    def prev_map(b, i):
        return (b, jnp.maximum(i * per - 1, 0), col_block)

    def next_map(b, i):
        return (b, jnp.minimum((i + 1) * per, last), col_block)

    return [pl.BlockSpec((1, HALO, width), prev_map),
            pl.BlockSpec((1, tm, width), lambda b, i: (b, i, col_block)),
            pl.BlockSpec((1, HALO, width), next_map)]


def _mod_kernel(c_ref, w_ref, b_ref, o_ref):
    c = c_ref[...]
    cs = c * jax.nn.sigmoid(c)
    o_ref[0] = jnp.dot(cs, w_ref[0], precision=lax.Precision.HIGHEST,
                       preferred_element_type=F32) + b_ref[0]


def _adaln_mod(c, ada_w, ada_b):
    depth, d, n = ada_w.shape
    b = c.shape[0]
    tn = n // 4
    return pl.pallas_call(
        _mod_kernel,
        grid=(depth, n // tn),
        in_specs=[pl.BlockSpec((b, d), lambda l, j: (0, 0)),
                  pl.BlockSpec((1, d, tn), lambda l, j: (l, 0, j)),
                  pl.BlockSpec((1, 1, tn), lambda l, j: (l, 0, j))],
        out_specs=pl.BlockSpec((1, b, tn), lambda l, j: (l, 0, j)),
        out_shape=jax.ShapeDtypeStruct((depth, b, n), F32),
        compiler_params=_cparams(("parallel", "parallel")),
        name="adaln_mod",
    )(c, ada_w, ada_b.reshape(depth, 1, n))


def _in_proj_kernel(x_ref, g_ref, sc_ref, sh_ref, w_ref, o_ref):
    h = _rmsnorm(x_ref[0], g_ref[...]) * (1.0 + sc_ref[0]) + sh_ref[0]
    o_ref[0] = _mm(h, w_ref[...]).astype(o_ref.dtype)


def _in_proj(x, g, sc, sh, w, out_dtype, tm=256):
    bsz, s, d = x.shape
    n = w.shape[1]
    vec = pl.BlockSpec((1, 1, d), lambda b, i: (b, 0, 0))
    return pl.pallas_call(
        _in_proj_kernel,
        grid=(bsz, s // tm),
        in_specs=[pl.BlockSpec((1, tm, d), lambda b, i: (b, i, 0)),
                  pl.BlockSpec((1, d), lambda b, i: (0, 0)), vec, vec,
                  _resident((d, n))],
        out_specs=pl.BlockSpec((1, tm, n), lambda b, i: (b, i, 0)),
        out_shape=jax.ShapeDtypeStruct((bsz, s, n), out_dtype),
        compiler_params=_cparams(("parallel", "parallel")),
        name="in_proj",
    )(x, g.reshape(1, d), sc.reshape(bsz, 1, d), sh.reshape(bsz, 1, d), w.astype(BF16))


def _out_proj_kernel(*refs, n_in):
    y_refs = refs[:n_in]
    w_refs = refs[n_in:2 * n_in]
    g_ref, gt_ref, x_ref, o_ref = refs[2 * n_in:]
    acc = _mm(y_refs[0][0], w_refs[0][...])
    for y_ref, w_ref in zip(y_refs[1:], w_refs[1:]):
        acc = acc + _mm(y_ref[0], w_ref[...])
    o_ref[0] = x_ref[0] + gt_ref[0] * _rmsnorm(acc, g_ref[...])


def _out_proj(ys, ws, g, gt, x, tm=512):
    bsz, s, d = x.shape
    n_in = len(ys)
    row = lambda width: pl.BlockSpec((1, tm, width), lambda b, i: (b, i, 0))
    return pl.pallas_call(
        functools.partial(_out_proj_kernel, n_in=n_in),
        grid=(bsz, s // tm),
        in_specs=([row(y.shape[-1]) for y in ys] + [_resident(w.shape) for w in ws]
                  + [pl.BlockSpec((1, d), lambda b, i: (0, 0)),
                     pl.BlockSpec((1, 1, d), lambda b, i: (b, 0, 0)), row(d)]),
        out_specs=row(d),
        out_shape=jax.ShapeDtypeStruct((bsz, s, d), F32),
        compiler_params=_cparams(("parallel", "parallel")),
        name="out_proj",
    )(*ys, *[w.astype(BF16) for w in ws], g.reshape(1, d), gt.reshape(bsz, 1, d), x)


def _ffn_kernel(xp_ref, x_ref, xn_ref, g0_ref, sc_ref, sh_ref, up_ref, cw_ref, dn_ref,
                g1_ref, gt_ref, o_ref, *, tm, d_ff, n_chunks):
    i = pl.program_id(1)
    last = pl.num_programs(1) - 1
    x = x_ref[0]
    xh = jnp.concatenate([xp_ref[0], x, xn_ref[0]], axis=0)
    h = _rmsnorm(xh, g0_ref[...]) * (1.0 + sc_ref[0]) + sh_ref[0]
    rows = lax.broadcasted_iota(jnp.int32, (tm + 2 * HALO, 1), 0)
    lo = jnp.where(i > 0, 0, HALO)
    hi = jnp.where(i < last, tm + 2 * HALO, tm + HALO)
    h = jnp.where((rows >= lo) & (rows < hi), h, 0.0).astype(BF16)
    fc = d_ff // n_chunks
    acc = jnp.zeros((tm, x.shape[1]), F32)
    for f in range(n_chunks):
        gsl = slice(f * fc, (f + 1) * fc)
        vsl = slice(d_ff + f * fc, d_ff + (f + 1) * fc)
        zg = _conv3_rows(_mm(h, up_ref[:, gsl]), cw_ref[:, gsl], tm)
        zv = _conv3_rows(_mm(h, up_ref[:, vsl]), cw_ref[:, vsl], tm)
        act = jax.nn.gelu(zg, approximate=True) * zv
        acc = acc + _mm(act, dn_ref[gsl, :])
    o_ref[0] = x + gt_ref[0] * _rmsnorm(acc, g1_ref[...])


def _ffn(x, g0, sc, sh, up, conv, down, g1, gt, tm=512, n_chunks=2):
    bsz, s, d = x.shape
    d_ff = down.shape[0]
    vec = pl.BlockSpec((1, 1, d), lambda b, i: (b, 0, 0))
    gvec = pl.BlockSpec((1, d), lambda b, i: (0, 0))
    return pl.pallas_call(
        functools.partial(_ffn_kernel, tm=tm, d_ff=d_ff, n_chunks=n_chunks),
        grid=(bsz, s // tm),
        in_specs=(_halo_specs(tm, d, s)
                  + [gvec, vec, vec, _resident(up.shape), _resident(conv.shape),
                     _resident(down.shape), gvec, vec]),
        out_specs=pl.BlockSpec((1, tm, d), lambda b, i: (b, i, 0)),
        out_shape=jax.ShapeDtypeStruct((bsz, s, d), F32),
        compiler_params=_cparams(("parallel", "parallel")),
        name="conv_glu_ffn",
    )(x, x, x, g0.reshape(1, d), sc.reshape(bsz, 1, d), sh.reshape(bsz, 1, d),
      up.astype(BF16), conv, down.astype(BF16), g1.reshape(1, d), gt.reshape(bsz, 1, d))


def _head_sum(x, ones_bd):
    hi, lo = _split(x, 2)
    return (jnp.dot(hi, ones_bd, preferred_element_type=F32)
            + jnp.dot(lo, ones_bd, preferred_element_type=F32))


def _rwkv_prep_kernel(zp_ref, z_ref, zn_ref, cw_ref, w1_ref, w2_ref, vec_ref, ones_ref,
                      r_o, v_o, kk_o, g_o, bonus_o, kd_o, b_o, lw_o, *, tm, width):
    i = pl.program_id(1)
    last = pl.num_programs(1) - 1
    zp = jnp.where(i > 0, zp_ref[0], 0.0)
    zn = jnp.where(i < last, zn_ref[0], 0.0)
    z = jnp.concatenate([zp, z_ref[0], zn], axis=0)
    za = _conv3_rows(z, cw_ref[...], tm)
    r = za[:, 0:width]
    k = za[:, width:2 * width]
    v = za[:, 2 * width:3 * width]
    u = za[:, 3 * width:4 * width]
    ones_bd = ones_ref[...]
    vec = vec_ref[...]
    k_k, k_a, r_k = vec[4:5], vec[5:6], vec[6:7]

    lora = _mm(u, w1_ref[...])
    g_o[0] = _mm(jax.nn.sigmoid(lora[:, 4 * LANES:5 * LANES]), w2_ref[4])
    kk = k * k_k
    kk = kk * lax.rsqrt(_head_sum(kk * kk, ones_bd) + L2_EPS)
    r_o[0] = r
    v_o[0] = v
    kk_o[0] = kk
    bonus = jnp.zeros_like(r)
    for d in range(2):
        q = vec[d:d + 1] + _mm(jnp.tanh(lora[:, d * LANES:(d + 1) * LANES]), w2_ref[d])
        w_log = jnp.minimum(q, 0.0) - jnp.log1p(jnp.exp(-jnp.abs(q))) - 0.5
        a =jax.nn.sigmoid(vec[2 + d:3 + d]
                           + _mm(lora[:, (2 + d) * LANES:(3 + d) * LANES], w2_ref[2 + d]))
        kd = k * (1.0 + (a - 1.0) * k_a)
        kd_o[d, 0] = kd
        b_o[d, 0] = kk * a
        lw_o[d, 0] = -jnp.exp(w_log)
        bonus = bonus + _head_sum(r * kd * r_k, ones_bd)
    bonus_o[0] = bonus * v


def _rwkv_prep(z, conv, w0, w1, w2, a0, a1, a2, g1, g2, k_k, k_a, r_k, ones_bd, tm=256):
    bsz, s, _ = z.shape
    width = k_k.shape[0]
    lora = w1.shape[-1]
    pad_c = lambda m: jnp.pad(m, ((0, 0), (0, LANES - m.shape[1])))
    pad_r = lambda m: jnp.pad(m, ((0, LANES - m.shape[0]), (0, 0)))
    assert lora <= LANES and g1.shape[1] == LANES
    w1cat = jnp.concatenate([pad_c(w1[0]), pad_c(w1[1]), pad_c(a1[0]), pad_c(a1[1]), g1],
                            axis=1).astype(BF16)
    w2cat = jnp.stack([pad_r(w2[0]), pad_r(w2[1]), pad_r(a2[0]), pad_r(a2[1]), g2]).astype(BF16)
    vec = jnp.stack([w0[0], w0[1], a0[0], a0[1], k_k, k_a, r_k.reshape(-1),
                     jnp.zeros_like(k_k)])
    out1 = jax.ShapeDtypeStruct((bsz, s, width), F32)
    out2 = jax.ShapeDtypeStruct((2, bsz, s, width), F32)
    spec1 = pl.BlockSpec((1, tm, width), lambda b, i: (b, i, 0))
    spec2 = pl.BlockSpec((2, 1, tm, width), lambda b, i: (0, b, i, 0))
    return pl.pallas_call(
        functools.partial(_rwkv_prep_kernel, tm=tm, width=width),
        grid=(bsz, s // tm),
        in_specs=(_halo_specs(tm, 4 * width, s)
                  + [_resident(conv.shape), _resident(w1cat.shape), _resident(w2cat.shape),
                     _resident(vec.shape), _resident(ones_bd.shape)]),
        out_specs=[spec1] * 5 + [spec2] * 3,
        out_shape=[out1] * 5 + [out2] * 3,
        compiler_params=_cparams(("parallel", "parallel")),
        name="rwkv_prep",
    )(z, z, z, conv, w1cat, w2cat, vec, ones_bd)


def _stack(x, head0):
    return jnp.concatenate([jnp.where(head0, x, 0.0), jnp.where(head0, 0.0, x)], axis=0)


def _wkv_pair(r, k, v, kk, b, lw, h0, tri, m_strict, m_incl, ones_c):
    c = r.shape[0]
    lw3 = _split(lw, 3)
    cum = sum(jnp.dot(tri, p, preferred_element_type=F32) for p in lw3)
    tot = jnp.sum(lw, axis=0, keepdims=True)
    tot_col = sum(lax.dot_general(p, ones_c, (((0,), (0,)), ((), ())),
                                  preferred_element_type=F32) for p in lw3)
    g_inv = jnp.exp(-cum)
    g_rat = jnp.exp(tot - cum)
    head0 = lax.broadcasted_iota(jnp.int32, (1, PAIR), 1) < HEAD_DIM
    kks = _stack(kk * jnp.exp(cum - lw), head0)
    rs = _stack(r * jnp.exp(cum), head0)
    khs = _stack(k * g_inv, head0)
    bhs = _stack(b * g_inv, head0)
    kgs = _stack(k * g_rat, head0)
    bgs = _stack(b * g_rat, head0)
    vs = _stack(v, head0)

    a_all = _mm_nt(jnp.concatenate([kks, rs], axis=0), jnp.concatenate([khs, bhs], axis=0))
    n2 = 2 * c
    akk = jnp.where(m_strict, a_all[:n2, :n2], 0.0)
    akb = jnp.where(m_strict, a_all[:n2, n2:], 0.0)
    ark = jnp.where(m_incl, a_all[n2:, :n2], 0.0)
    arb = jnp.where(m_incl, a_all[n2:, n2:], 0.0)

    eye = (lax.broadcasted_iota(jnp.int32, (n2, n2), 0)
           == lax.broadcasted_iota(jnp.int32, (n2, n2), 1)).astype(F32)
    p = -akb
    t = eye + p
    steps = int(np.ceil(np.log2(c))) - 1
    for _ in range(steps):
        p = _mm(p, p)
        t = t + _mm(t, p)
    ia = (eye + akb).astype(BF16)
    res = eye - sum(jnp.dot(ia, tp, preferred_element_type=F32) for tp in _split(t, 2))
    t = t + _mm(t, res)

    w12 = _mm(t, jnp.concatenate([kks, _mm(akk, vs)], axis=1))
    w1s = w12[:, :PAIR]
    w2s = w12[:, PAIR:]
    ps = rs - _mm(arb, w1s)
    y1s = _mm(ark, vs) - _mm(arb, w2s)
    ys = _mm(ps, h0) + y1s
    y = ys[:c] + ys[c:]
    h_new = (jnp.exp(tot_col) * h0 - _mm(_mm_tn(bgs, w1s), h0)
             + _mm_tn(kgs, vs) - _mm_tn(bgs, w2s))
    return y, h_new


def _wkv_kernel(r_ref, v_ref, kk_ref, k_ref, b_ref, lw_ref, tri_ref, ms_ref, mi_ref, ones_ref,
                y_ref, h_ref, *, n_pairs):
    @pl.when(pl.program_id(2) == 0)
    def _():
        h_ref[...] = jnp.zeros_like(h_ref)

    tri = tri_ref[0]
    m_strict = ms_ref[0] > 0.5
    m_incl = mi_ref[0] > 0.5
    ones_c = ones_ref[...]
    for p in range(n_pairs):
        sl = slice(p * PAIR, (p + 1) * PAIR)
        y, h_new = _wkv_pair(r_ref[0, :, sl], k_ref[0, 0, :, sl], v_ref[0, :, sl],
                             kk_ref[0, :, sl], b_ref[0, 0, :, sl], lw_ref[0, 0, :, sl],
                             h_ref[p], tri, m_strict, m_incl, ones_c)
        y_ref[0, 0, :, sl] = y
        h_ref[p] = h_new


def _wkv_scan(r, v, kk, kd, bb, lw):
    bsz, s, width = r.shape
    c = CHUNK
    nc = s // c
    n_pairs = width // PAIR
    t_idx = np.arange(c)
    before = np.stack([t_idx[None, :] < t_idx[:, None], t_idx[None, :] > t_idx[:, None]])
    eye = np.eye(c, dtype=bool)[None]
    blockdiag = lambda m: np.stack([np.kron(np.eye(2, dtype=bool), m[d]) for d in range(2)])
    tri = jnp.asarray(before | eye, BF16)
    m_strict = jnp.asarray(blockdiag(before), F32)
    m_incl = jnp.asarray(blockdiag(before | eye), F32)
    ones_c = jnp.ones((c, PAIR), BF16)

    def chunk_idx(d, ci):
        return ci + d * (nc - 1 - 2 * ci)

    shared = pl.BlockSpec((1, c, width), lambda b, d, ci: (b, chunk_idx(d, ci), 0))
    per_dir = pl.BlockSpec((1, 1, c, width), lambda b, d, ci: (d, b, chunk_idx(d, ci), 0))
    const3 = lambda n: pl.BlockSpec((1, n, n), lambda b, d, ci: (d, 0, 0))
    return pl.pallas_call(
        functools.partial(_wkv_kernel, n_pairs=n_pairs),
        grid=(bsz, 2, nc),
        in_specs=[shared, shared, shared, per_dir, per_dir, per_dir,
                  const3(c), const3(2 * c), const3(2 * c),
                  pl.BlockSpec((c, PAIR), lambda b, d, ci: (0, 0))],
        out_specs=per_dir,
        out_shape=jax.ShapeDtypeStruct((2, bsz, s, width), F32),
        scratch_shapes=[pltpu.VMEM((n_pairs, PAIR, PAIR), F32)],
        compiler_params=_cparams(("parallel", "parallel", "arbitrary")),
        name="wkv_scan",
    )(r, v, kk, kd, bb, lw, tri, m_strict, m_incl, ones_c)


def _rwkv_post_kernel(y0_ref, y1_ref, g_ref, bonus_ref, lw_ref, lb_ref, ones_ref, o_ref):
    ones_bd = ones_ref[...]
    y = y0_ref[0, 0] + y1_ref[0, 0]
    inv_n = 1.0 / HEAD_DIM
    yc = y - _head_sum(y, ones_bd) * inv_n
    var = _head_sum(yc * yc, ones_bd) * inv_n
    yn = yc * lax.rsqrt(var + LNX_EPS) * lw_ref[...] + lb_ref[...]
    o_ref[0] = ((yn + bonus_ref[0]) * g_ref[0]).astype(o_ref.dtype)


def _rwkv_post(y, g, bonus, lnx_w, lnx_b, ones_bd, tm=512):
    _, bsz, s, width = y.shape
    row = pl.BlockSpec((1, tm, width), lambda b, i: (b, i, 0))
    ydir = lambda d: pl.BlockSpec((1, 1, tm, width), lambda b, i: (d, b, i, 0))
    vecs = pl.BlockSpec((1, width), lambda b, i: (0, 0))
    return pl.pallas_call(
        _rwkv_post_kernel,
        grid=(bsz, s // tm),
        in_specs=[ydir(0), ydir(1), row, row, vecs, vecs, _resident(ones_bd.shape)],
        out_specs=row,
        out_shape=jax.ShapeDtypeStruct((bsz, s, width), F32),
        compiler_params=_cparams(("parallel", "parallel")),
        name="rwkv_post",
    )(y, y, g, bonus, lnx_w.reshape(1, width), lnx_b.reshape(1, width), ones_bd)


def _rope_rows(x, cos, sin_signed):
    half = HEAD_DIM // 2
    lane = lax.broadcasted_iota(jnp.int32, (1, PAIR), 1)
    first_half = (lane % HEAD_DIM) < half
    partner = jnp.where(first_half, pltpu.roll(x, PAIR - half, axis=1), pltpu.roll(x, half, axis=1))
    return x * cos + partner * sin_signed


def _dilated_kernel(q_ref, k_ref, v_ref, cos_ref, sin_ref, o_ref,
                    qr, kp, vp, m_s, den_s, num_s, *, seq, pad, rope_rows):
    scale = HEAD_DIM ** -0.5
    zeros_pad = jnp.zeros((pad, PAIR), F32)
    kp[0:pad] = zeros_pad
    vp[0:pad] = zeros_pad
    kp[pad + seq:pad + seq + pad] = zeros_pad
    vp[pad + seq:pad + seq + pad] = zeros_pad

    def rope_body(j, carry):
        rows = pl.ds(pl.multiple_of(j * rope_rows, rope_rows), rope_rows)
        cos = cos_ref[rows, :]
        sin = sin_ref[rows, :]
        qr[rows, :] = _rope_rows(q_ref[0, rows, :], cos, sin) * scale
        dst = pl.ds(pl.multiple_of(pad + j * rope_rows, rope_rows), rope_rows)
        kp[dst, :] = _rope_rows(k_ref[0, rows, :], cos, sin)
        vp[dst, :] = v_ref[0, rows, :]
        return carry

    lax.fori_loop(0, seq // rope_rows, rope_body, 0)

    lane = lax.broadcasted_iota(jnp.int32, (1, PAIR), 1)
    head0 = lane < HEAD_DIM
    qi = lax.broadcasted_iota(jnp.int32, (QBLK, 1), 0)
    kj = lax.broadcasted_iota(jnp.int32, (1, KBLK), 1)
    half_w = QBLK // 2
    band = jnp.abs(kj - half_w - qi) <= half_w

    for branch, (window, dil) in enumerate(DILATED_PAIRS):
        assert window // (2 * dil) == half_w
        m_len = seq // dil
        blocks_per_res = m_len // QBLK

        def block_body(it, carry, dil=dil, m_len=m_len, blocks_per_res=blocks_per_res,
                       first=(branch == 0)):
            res = it // blocks_per_res
            m0 = (it % blocks_per_res) * QBLK
            q_start = res + dil * m0
            k_start = pad + res + dil * (m0 - half_w)
            if dil == 1:
                q_rows = pl.ds(pl.multiple_of(q_start, QBLK), QBLK)
                k_rows = pl.ds(pl.multiple_of(k_start, half_w), KBLK)
            else:
                q_rows = pl.ds(q_start, QBLK, stride=dil)
                k_rows = pl.ds(k_start, KBLK, stride=dil)
            qb = qr[q_rows, :]
            kb = kp[k_rows, :].astype(BF16)
            vb = vp[k_rows, :].astype(BF16)
            kpos = m0 - half_w + kj
            valid = band & (kpos >= 0) & (kpos < m_len)
            heads = []
            for h in range(2):
                qh = jnp.where(head0 if h == 0 else ~head0, qb, 0.0)
                sc = jnp.where(valid, _mm_nt(qh, kb), NEG_BIG)
                mh = jnp.max(sc, axis=-1, keepdims=True)
                ph = jnp.exp(sc - mh)
                dh = jnp.sum(ph, axis=-1, keepdims=True)
                heads.append((mh, dh, _mm(ph, vb)))
            m_b = jnp.where(head0, heads[0][0], heads[1][0])
            d_b = jnp.where(head0, heads[0][1], heads[1][1])
            n_b = jnp.where(head0, heads[0][2], heads[1][2])
            if first:
                m_s[q_rows, :] = m_b
                den_s[q_rows, :] = d_b
                num_s[q_rows, :] = n_b
            else:
                m_old = m_s[q_rows, :]
                m_new = jnp.maximum(m_old, m_b)
                alpha = jnp.exp(m_old - m_new)
                beta = jnp.exp(m_b - m_new)
                m_s[q_rows, :] = m_new
                den_s[q_rows, :] = alpha * den_s[q_rows, :] + beta * d_b
                num_s[q_rows, :] = alpha * num_s[q_rows, :] + beta * n_b
            return carry

        lax.fori_loop(0, seq // QBLK, block_body, 0)

    def out_body(j, carry):
        rows = pl.ds(pl.multiple_of(j * rope_rows, rope_rows), rope_rows)
        o_ref[0, rows, :] = (num_s[rows, :] / den_s[rows, :]).astype(o_ref.dtype)
        return carry

    lax.fori_loop(0, seq // rope_rows, out_body, 0)


def _dilated_attention(z, col0, width):
    bsz, s, _ = z.shape
    n_pairs = width // PAIR
    max_dil = max(d for _, d in DILATED_PAIRS)
    assert s % (QBLK * max_dil) == 0 and col0 % PAIR == 0
    pad = (QBLK // 2) * max_dil
    half = HEAD_DIM // 2
    inv = ROPE_THETA ** (-jnp.arange(half, dtype=F32) / half)
    ang = jnp.arange(s, dtype=F32)[:, None] * inv[None, :]
    cos = jnp.tile(jnp.cos(ang), (1, PAIR // half))
    sin = jnp.tile(jnp.concatenate([-jnp.sin(ang), jnp.sin(ang)], axis=1), (1, 2))
    cb = col0 // PAIR
    col = lambda off: pl.BlockSpec((1, s, PAIR), lambda b, p: (b, 0, cb + off * n_pairs + p))
    rope_rows = 512
    return pl.pallas_call(
        functools.partial(_dilated_kernel, seq=s, pad=pad, rope_rows=rope_rows),
        grid=(bsz, n_pairs),
        in_specs=[col(0), col(1), col(2), _resident((s, PAIR)), _resident((s, PAIR))],
        out_specs=pl.BlockSpec((1, s, PAIR), lambda b, p: (b, 0, p)),
        out_shape=jax.ShapeDtypeStruct((bsz, s, width), F32),
        scratch_shapes=[pltpu.VMEM((s, PAIR), F32),
                        pltpu.VMEM((s + 2 * pad, PAIR), F32),
                        pltpu.VMEM((s + 2 * pad, PAIR), F32),
                        pltpu.VMEM((s, PAIR), F32),
                        pltpu.VMEM((s, PAIR), F32),
                        pltpu.VMEM((s, PAIR), F32)],
        compiler_params=_cparams(("parallel", "parallel")),
        name="dilated_attention",
    )(z, z, z, cos, sin)


def _seq_dft_kernel(a_ref, x_ref, o_ref, acc_ref):
    kk = pl.program_id(3)

    @pl.when(kk == 0)
    def _():
        acc_ref[...] = jnp.zeros_like(acc_ref)

    acc_ref[...] += jnp.dot(a_ref[...], x_ref[0], preferred_element_type=F32)

    @pl.when(kk == pl.num_programs(3) - 1)
    def _():
        o_ref[0] = acc_ref[...].astype(o_ref.dtype)


def _seq_dft(a_cat, x2, tm=1024, tn=1024, tk=512):
    bsz, s, two_d = x2.shape
    d = two_d // 2
    nks = s // tk
    return pl.pallas_call(
        _seq_dft_kernel,
        grid=(bsz, s // tm, d // tn, 2 * nks),
        in_specs=[pl.BlockSpec((tm, tk), lambda b, i, j, kk: (i, kk)),
                  pl.BlockSpec((1, tk, tn),
                               lambda b, i, j, kk: (b, kk % nks, (kk // nks) * (d // tn) + j))],
        out_specs=pl.BlockSpec((1, tm, tn), lambda b, i, j, kk: (b, i, j)),
        out_shape=jax.ShapeDtypeStruct((bsz, s, d), BF16),
        scratch_shapes=[pltpu.VMEM((tm, tn), F32)],
        compiler_params=_cparams(("parallel", "parallel", "parallel", "arbitrary")),
        name="seq_dft",
    )(a_cat, x2)


def _dft_tables(s, d):
    gsz = d // N_FOURIER_GROUPS
    cidx = jnp.arange(gsz, dtype=jnp.int32)
    ang_c = (2.0 * np.pi / gsz) * ((cidx[:, None] * cidx[None, :]) % gsz).astype(F32)
    eye_g = jnp.eye(N_FOURIER_GROUPS, dtype=F32)
    wc = jnp.kron(eye_g, jnp.cos(ang_c)) * (gsz ** -0.5)
    ws = jnp.kron(eye_g, jnp.sin(ang_c)) * (gsz ** -0.5)
    w_ch = jnp.concatenate([wc, ws], axis=1)
    sidx = jnp.arange(s, dtype=jnp.int32)
    ang_s = (2.0 * np.pi / s) * ((sidx[:, None] * sidx[None, :]) % s).astype(F32)
    a_cat = jnp.concatenate([jnp.cos(ang_s), -jnp.sin(ang_s)], axis=1) * (s ** -0.5)
    return w_ch.astype(BF16), a_cat.astype(BF16)


def kernel(x, c, ada_w, ada_b, norm_g, mix_in, mix_conv, rwkv_w0, rwkv_w1, rwkv_w2, rwkv_a0, rwkv_a1, rwkv_a2, rwkv_g1, rwkv_g2, rwkv_k_k, rwkv_k_a, rwkv_r_k, rwkv_lnx_w, rwkv_lnx_b, mix_out, fnet_w, ffn_up, ffn_conv, ffn_down):
    bsz, s, d = x.shape
    depth = ada_w.shape[0]
    width_a = rwkv_k_k.shape[-1]
    width_b = mix_out.shape[1] - width_a
    in_cols_a = 4 * width_a
    ones_bd = jnp.kron(jnp.eye(width_a // HEAD_DIM, dtype=F32),
                       jnp.ones((HEAD_DIM, HEAD_DIM), F32)).astype(BF16)

    mod = _adaln_mod(c, ada_w, ada_b)
    for l in range(depth):
        sh1, sc1, gt1, sh2, sc2, gt2 = [mod[l, :, j * d:(j + 1) * d] for j in range(6)]
        if l % 2 == 0:
            e = l // 2
            z = _in_proj(x, norm_g[l, 0], sc1, sh1, mix_in[e], F32)
            r, v, kk, g, bonus, kd, bb, lw = _rwkv_prep(
                z, mix_conv[e], rwkv_w0[e], rwkv_w1[e], rwkv_w2[e], rwkv_a0[e], rwkv_a1[e],
                rwkv_a2[e], rwkv_g1[e], rwkv_g2[e], rwkv_k_k[e], rwkv_k_a[e], rwkv_r_k[e], ones_bd)
            y = _wkv_scan(r, v, kk, kd, bb, lw)
            ya = _rwkv_post(y, g, bonus, rwkv_lnx_w[e], rwkv_lnx_b[e], ones_bd)
            yb = _dilated_attention(z, in_cols_a, width_b)
            x = _out_proj([ya, yb], [mix_out[e, :width_a], mix_out[e, width_a:]],
                          norm_g[l, 1], gt1, x)
        else:
            w_ch, a_cat = _dft_tables(s, d)
            x2 = _in_proj(x, norm_g[l, 0], sc1, sh1, w_ch, BF16)
            f = _seq_dft(a_cat, x2)
            x = _out_proj([f], [fnet_w[l // 2]], norm_g[l, 1], gt1, x)
        x = _ffn(x, norm_g[l, 2], sc2, sh2, ffn_up[l], ffn_conv[l], ffn_down[l],
                 norm_g[l, 3], gt2)
    return x
```

```python
import functools

import jax
import jax.numpy as jnp
import numpy as np
from jax import lax
from jax.experimental import pallas as pl
from jax.experimental.pallas import tpu as pltpu

F32 = jnp.float32
BF16 = jnp.bfloat16

HEAD_DIM = 64
DILATED_PAIRS = ((128, 1), (512, 4), (2048, 16))
ROPE_THETA = 10000.0
N_FOURIER_GROUPS = 8
RMS_EPS = 1e-6
LNX_EPS = 64e-5
L2_EPS = 1e-12
NEG_BIG = -1e30

LANES = 128
SUBLANES = 8
VMEM_LIMIT_BYTES = 56 * 1024 * 1024

PAIR = 2 * HEAD_DIM
assert PAIR == LANES
HALO = SUBLANES
CHUNK = 64
QBLK = 128
KBLK = 2 * QBLK


def _cparams(sem):
    return pltpu.CompilerParams(dimension_semantics=sem, vmem_limit_bytes=VMEM_LIMIT_BYTES)


def _resident(shape):
    zeros = (0,) * len(shape)
    return pl.BlockSpec(shape, lambda *_: zeros, pipeline_mode=pl.Buffered(1))


def _mm(a, b):
    return jnp.dot(a.astype(BF16), b.astype(BF16), preferred_element_type=F32)


def _mm_nt(a, b):
    return lax.dot_general(a.astype(BF16), b.astype(BF16), (((1,), (1,)), ((), ())),
                           preferred_element_type=F32)


def _split(x, n):
    parts = []
    rem = x
    for _ in range(n):
        p = rem.astype(BF16)
        parts.append(p)
        rem = rem - p.astype(F32)
    return parts


def _rmsnorm(x, g):
    return x * lax.rsqrt(jnp.mean(x * x, -1, keepdims=True) + RMS_EPS) * g


def _conv3_rows(z, w, n_rows):
    total = z.shape[0]
    y = pltpu.roll(z, 1, axis=0) * w[0:1] + z * w[1:2] + pltpu.roll(z, total - 1, axis=0) * w[2:3]
    return y[HALO:HALO + n_rows]


def _halo_specs(tm, width, seq, col_block=0):
    per = tm // HALO
    last = seq // HALO - 1

    def prev_map(b, i):
        return (b, jnp.maximum(i * per - 1, 0), col_block)

    def next_map(b, i):
        return (b, jnp.minimum((i + 1) * per, last), col_block)

    return [pl.BlockSpec((1, HALO, width), prev_map),
            pl.BlockSpec((1, tm, width), lambda b, i: (b, i, col_block)),
            pl.BlockSpec((1, HALO, width), next_map)]


def _mod_kernel(c_ref, w_ref, b_ref, o_ref):
    c = c_ref[...]
    cs = c * jax.nn.sigmoid(c)
    o_ref[0] = jnp.dot(cs, w_ref[0], precision=lax.Precision.HIGHEST,
                       preferred_element_type=F32) + b_ref[0]


def _adaln_mod(c, ada_w, ada_b):
    depth, d, n = ada_w.shape
    b = c.shape[0]
    tn = n // 4
    return pl.pallas_call(
        _mod_kernel,
        grid=(depth, n // tn),
        in_specs=[pl.BlockSpec((b, d), lambda l, j: (0, 0)),
                  pl.BlockSpec((1, d, tn), lambda l, j: (l, 0, j)),
                  pl.BlockSpec((1, 1, tn), lambda l, j: (l, 0, j))],
        out_specs=pl.BlockSpec((1, b, tn), lambda l, j: (l, 0, j)),
        out_shape=jax.ShapeDtypeStruct((depth, b, n), F32),
        compiler_params=_cparams(("parallel", "parallel")),
        name="adaln_mod",
    )(c, ada_w, ada_b.reshape(depth, 1, n))


def _in_proj_kernel(x_ref, g_ref, sc_ref, sh_ref, w_ref, o_ref):
    h = _rmsnorm(x_ref[0], g_ref[...]) * (1.0 + sc_ref[0]) + sh_ref[0]
    o_ref[0] = _mm(h, w_ref[...]).astype(o_ref.dtype)


def _in_proj(x, g, sc, sh, w, out_dtype, tm=256):
    bsz, s, d = x.shape
    n = w.shape[1]
    vec = pl.BlockSpec((1, 1, d), lambda b, i: (b, 0, 0))
    return pl.pallas_call(
        _in_proj_kernel,
        grid=(bsz, s // tm),
        in_specs=[pl.BlockSpec((1, tm, d), lambda b, i: (b, i, 0)),
                  pl.BlockSpec((1, d), lambda b, i: (0, 0)), vec, vec,
                  _resident((d, n))],
        out_specs=pl.BlockSpec((1, tm, n), lambda b, i: (b, i, 0)),
        out_shape=jax.ShapeDtypeStruct((bsz, s, n), out_dtype),
        compiler_params=_cparams(("parallel", "parallel")),
        name="in_proj",
    )(x, g.reshape(1, d), sc.reshape(bsz, 1, d), sh.reshape(bsz, 1, d), w.astype(BF16))


def _out_proj_kernel(*refs, n_in):
    y_refs = refs[:n_in]
    w_refs = refs[n_in:2 * n_in]
    g_ref, gt_ref, x_ref, o_ref = refs[2 * n_in:]
    acc = _mm(y_refs[0][0], w_refs[0][...])
    for y_ref, w_ref in zip(y_refs[1:], w_refs[1:]):
        acc = acc + _mm(y_ref[0], w_ref[...])
    o_ref[0] = x_ref[0] + gt_ref[0] * _rmsnorm(acc, g_ref[...])


def _out_proj(ys, ws, g, gt, x, tm=512):
    bsz, s, d = x.shape
    n_in = len(ys)
    row = lambda width: pl.BlockSpec((1, tm, width), lambda b, i: (b, i, 0))
    return pl.pallas_call(
        functools.partial(_out_proj_kernel, n_in=n_in),
        grid=(bsz, s // tm),
        in_specs=([row(y.shape[-1]) for y in ys] + [_resident(w.shape) for w in ws]
                  + [pl.BlockSpec((1, d), lambda b, i: (0, 0)),
                     pl.BlockSpec((1, 1, d), lambda b, i: (b, 0, 0)), row(d)]),
        out_specs=row(d),
        out_shape=jax.ShapeDtypeStruct((bsz, s, d), F32),
        compiler_params=_cparams(("parallel", "parallel")),
        name="out_proj",
    )(*ys, *[w.astype(BF16) for w in ws], g.reshape(1, d), gt.reshape(bsz, 1, d), x)


def _ffn_kernel(xp_ref, x_ref, xn_ref, g0_ref, sc_ref, sh_ref, up_ref, cw_ref, dn_ref,
                g1_ref, gt_ref, o_ref, *, tm, d_ff, n_chunks):
    i = pl.program_id(1)
    last = pl.num_programs(1) - 1
    x = x_ref[0]
    xh = jnp.concatenate([xp_ref[0], x, xn_ref[0]], axis=0)
    h = _rmsnorm(xh, g0_ref[...]) * (1.0 + sc_ref[0]) + sh_ref[0]
    rows = lax.broadcasted_iota(jnp.int32, (tm + 2 * HALO, 1), 0)
    lo = jnp.where(i > 0, 0, HALO)
    hi = jnp.where(i < last, tm + 2 * HALO, tm + HALO)
    h = jnp.where((rows >= lo) & (rows < hi), h, 0.0).astype(BF16)
    fc = d_ff // n_chunks
    acc = jnp.zeros((tm, x.shape[1]), F32)
    for f in range(n_chunks):
        gsl = slice(f * fc, (f + 1) * fc)
        vsl = slice(d_ff + f * fc, d_ff + (f + 1) * fc)
        zg = _conv3_rows(_mm(h, up_ref[:, gsl]), cw_ref[:, gsl], tm)
        zv = _conv3_rows(_mm(h, up_ref[:, vsl]), cw_ref[:, vsl], tm)
        act = jax.nn.gelu(zg, approximate=True) * zv
        acc = acc + _mm(act, dn_ref[gsl, :])
    o_ref[0] = x + gt_ref[0] * _rmsnorm(acc, g1_ref[...])


def _ffn(x, g0, sc, sh, up, conv, down, g1, gt, tm=512, n_chunks=2):
    bsz, s, d = x.shape
    d_ff = down.shape[0]
    vec = pl.BlockSpec((1, 1, d), lambda b, i: (b, 0, 0))
    gvec = pl.BlockSpec((1, d), lambda b, i: (0, 0))
    return pl.pallas_call(
        functools.partial(_ffn_kernel, tm=tm, d_ff=d_ff, n_chunks=n_chunks),
        grid=(bsz, s // tm),
        in_specs=(_halo_specs(tm, d, s)
                  + [gvec, vec, vec, _resident(up.shape), _resident(conv.shape),
                     _resident(down.shape), gvec, vec]),
        out_specs=pl.BlockSpec((1, tm, d), lambda b, i: (b, i, 0)),
        out_shape=jax.ShapeDtypeStruct((bsz, s, d), F32),
        compiler_params=_cparams(("parallel", "parallel")),
        name="conv_glu_ffn",
    )(x, x, x, g0.reshape(1, d), sc.reshape(bsz, 1, d), sh.reshape(bsz, 1, d),
      up.astype(BF16), conv, down.astype(BF16), g1.reshape(1, d), gt.reshape(bsz, 1, d))


def _head_sum(x, ones_bd):
    hi, lo = _split(x, 2)
    return (jnp.dot(hi, ones_bd, preferred_element_type=F32)
            + jnp.dot(lo, ones_bd, preferred_element_type=F32))


def _rwkv_prep_kernel(zp_ref, z_ref, zn_ref, cw_ref, w1_ref, w2_ref, vec_ref, ones_ref,
                      r_o, v_o, kk_o, g_o, bonus_o, kd_o, b_o, lw_o, *, tm, width):
    i = pl.program_id(1)
    last = pl.num_programs(1) - 1
    zp = jnp.where(i > 0, zp_ref[0], 0.0)
    zn = jnp.where(i < last, zn_ref[0], 0.0)
    z = jnp.concatenate([zp, z_ref[0], zn], axis=0)
    za = _conv3_rows(z, cw_ref[...], tm)
    r = za[:, 0:width]
    k = za[:, width:2 * width]
    v = za[:, 2 * width:3 * width]
    u = za[:, 3 * width:4 * width]
    ones_bd = ones_ref[...]
    vec = vec_ref[...]
    k_k, k_a, r_k = vec[4:5], vec[5:6], vec[6:7]

    lora = _mm(u, w1_ref[...])
    g_o[0] = _mm(jax.nn.sigmoid(lora[:, 4 * LANES:5 * LANES]), w2_ref[4])
    kk = k * k_k
    kk = kk * lax.rsqrt(_head_sum(kk * kk, ones_bd) + L2_EPS)
    r_o[0] = r
    v_o[0] = v
    kk_o[0] = kk
    bonus = jnp.zeros_like(r)
    for d in range(2):
        q = vec[d:d + 1] + _mm(jnp.tanh(lora[:, d * LANES:(d + 1) * LANES]), w2_ref[d])
        w_log = jnp.minimum(q, 0.0) - jnp.log1p(jnp.exp(-jnp.abs(q))) - 0.5
        a =jax.nn.sigmoid(vec[2 + d:3 + d]
                           + _mm(lora[:, (2 + d) * LANES:(3 + d) * LANES], w2_ref[2 + d]))
        kd = k * (1.0 + (a - 1.0) * k_a)
        kd_o[d, 0] = kd
        b_o[d, 0] = kk * a
        lw_o[d, 0] = -jnp.exp(w_log)
        bonus = bonus + _head_sum(r * kd * r_k, ones_bd)
    bonus_o[0] = bonus * v


def _rwkv_prep(z, conv, w0, w1, w2, a0, a1, a2, g1, g2, k_k, k_a, r_k, ones_bd, tm=256):
    bsz, s, _ = z.shape
    width = k_k.shape[0]
    lora = w1.shape[-1]
    pad_c = lambda m: jnp.pad(m, ((0, 0), (0, LANES - m.shape[1])))
    pad_r = lambda m: jnp.pad(m, ((0, LANES - m.shape[0]), (0, 0)))
    assert lora <= LANES and g1.shape[1] == LANES
    w1cat = jnp.concatenate([pad_c(w1[0]), pad_c(w1[1]), pad_c(a1[0]), pad_c(a1[1]), g1],
                            axis=1).astype(BF16)
    w2cat = jnp.stack([pad_r(w2[0]), pad_r(w2[1]), pad_r(a2[0]), pad_r(a2[1]), g2]).astype(BF16)
    vec = jnp.stack([w0[0], w0[1], a0[0], a0[1], k_k, k_a, r_k.reshape(-1),
                     jnp.zeros_like(k_k)])
    out1 = jax.ShapeDtypeStruct((bsz, s, width), F32)
    out2 = jax.ShapeDtypeStruct((2, bsz, s, width), F32)
    spec1 = pl.BlockSpec((1, tm, width), lambda b, i: (b, i, 0))
    spec2 = pl.BlockSpec((2, 1, tm, width), lambda b, i: (0, b, i, 0))
    return pl.pallas_call(
        functools.partial(_rwkv_prep_kernel, tm=tm, width=width),
        grid=(bsz, s // tm),
        in_specs=(_halo_specs(tm, 4 * width, s)
                  + [_resident(conv.shape), _resident(w1cat.shape), _resident(w2cat.shape),
                     _resident(vec.shape), _resident(ones_bd.shape)]),
        out_specs=[spec1] * 5 + [spec2] * 3,
        out_shape=[out1] * 5 + [out2] * 3,
        compiler_params=_cparams(("parallel", "parallel")),
        name="rwkv_prep",
    )(z, z, z, conv, w1cat, w2cat, vec, ones_bd)


_BNN = (((2,), (1,)), ((0,), (0,)))
_BNT = (((2,), (2,)), ((0,), (0,)))
_BTN = (((1,), (1,)), ((0,), (0,)))


def _bmm(a, b, dims=_BNN):
    return lax.dot_general(a.astype(BF16), b.astype(BF16), dims, preferred_element_type=F32)


def _stack(x, head0):
    return jnp.concatenate([jnp.where(head0, x, 0.0), jnp.where(head0, 0.0, x)], axis=1)


def _wkv_group(r, k, v, kk, b, lw, h0, tri, m_strict, m_incl):
    grp, c, _ = r.shape
    n2 = 2 * c
    lw3 = _split(lw, 3)
    cum = sum(lax.dot_general(tri, p, _BNN, preferred_element_type=F32) for p in lw3)
    tot = jnp.sum(lw, axis=1, keepdims=True)
    ones_c = jnp.ones((grp, c, PAIR), BF16)
    tot_col = sum(lax.dot_general(p, ones_c, _BTN, preferred_element_type=F32) for p in lw3)
    g_inv = jnp.exp(-cum)
    g_rat = jnp.exp(tot - cum)
    head0 = lax.broadcasted_iota(jnp.int32, (1, 1, PAIR), 2) < HEAD_DIM
    kks = _stack(kk * jnp.exp(cum - lw), head0)
    rs = _stack(r * jnp.exp(cum), head0)
    khs = _stack(k * g_inv, head0)
    bhs = _stack(b * g_inv, head0)
    kgs = _stack(k * g_rat, head0)
    bgs = _stack(b * g_rat, head0)
    vs = _stack(v, head0)

    a_all = _bmm(jnp.concatenate([kks, rs], axis=1), jnp.concatenate([khs, bhs], axis=1), _BNT)
    akk = jnp.where(m_strict, a_all[:, :n2, :n2], 0.0)
    akb = jnp.where(m_strict, a_all[:, :n2, n2:], 0.0)
    ark = jnp.where(m_incl, a_all[:, n2:, :n2], 0.0)
    arb = jnp.where(m_incl, a_all[:, n2:, n2:], 0.0)

    eye = (lax.broadcasted_iota(jnp.int32, (1, n2, n2), 1)
           == lax.broadcasted_iota(jnp.int32, (1, n2, n2), 2)).astype(F32)
    p = -akb
    t = eye + p
    steps = int(np.ceil(np.log2(c))) - 1
    for _ in range(steps):
        p = _bmm(p, p)
        t = t + _bmm(t, p)
    ia = (eye + akb).astype(BF16)
    res = eye - sum(lax.dot_general(ia, tp, _BNN, preferred_element_type=F32)
                    for tp in _split(t, 2))
    t = t + _bmm(t, res)

    w12 = _bmm(t, jnp.concatenate([kks, _bmm(akk, vs)], axis=2))
    w1s = w12[:, :, :PAIR]
    w2s = w12[:, :, PAIR:]
    ps = rs - _bmm(arb, w1s)
    y1s = _bmm(ark, vs) - _bmm(arb, w2s)
    ys = _bmm(ps, h0) + y1s
    y = ys[:, :c] + ys[:, c:]
    h_new = (jnp.exp(tot_col) * h0 - _bmm(_bmm(bgs, w1s, _BTN), h0)
             + _bmm(kgs, vs, _BTN) - _bmm(bgs, w2s, _BTN))
    return y, h_new


def _wkv_kernel(r0_ref, r1_ref, v0_ref, v1_ref, kk0_ref, kk1_ref, k0_ref, k1_ref, b0_ref, b1_ref,
                lw0_ref, lw1_ref, tri_ref, ms_ref, mi_ref, y0_ref, y1_ref, h_ref, *, n_pairs):
    @pl.when(pl.program_id(1) == 0)
    def _():
        h_ref[...] = jnp.zeros_like(h_ref)

    lanes = [slice(p * PAIR, (p + 1) * PAIR) for p in range(n_pairs)]

    def group(ref0, ref1):
        return jnp.stack([ref0[0, :, sl] for sl in lanes] + [ref1[0, :, sl] for sl in lanes])

    per_dir = lambda ref: jnp.stack([ref[0]] * n_pairs + [ref[1]] * n_pairs)
    y, h_new = _wkv_group(group(r0_ref, r1_ref), group(k0_ref, k1_ref), group(v0_ref, v1_ref),
                          group(kk0_ref, kk1_ref), group(b0_ref, b1_ref), group(lw0_ref, lw1_ref),
                          h_ref[...], per_dir(tri_ref), per_dir(ms_ref) > 0.5, per_dir(mi_ref) > 0.5)
    h_ref[...] = h_new
    for p, sl in enumerate(lanes):
        y0_ref[0, :, sl] = y[p]
        y1_ref[0, :, sl] = y[n_pairs + p]


def _wkv_scan(r, v, kk, kd, bb, lw):
    bsz, s, width = r.shape
    c = CHUNK
    nc = s // c
    n_pairs = width // PAIR
    t_idx = np.arange(c)
    before = np.stack([t_idx[None, :] < t_idx[:, None], t_idx[None, :] > t_idx[:, None]])
    eye = np.eye(c, dtype=bool)[None]
    blockdiag = lambda m: np.stack([np.kron(np.eye(2, dtype=bool), m[d]) for d in range(2)])
    tri = jnp.asarray(before | eye, BF16)
    m_strict = jnp.asarray(blockdiag(before), F32)
    m_incl = jnp.asarray(blockdiag(before | eye), F32)

    fwd = pl.BlockSpec((1, c, width), lambda b, ci: (b, ci, 0))
    bwd = pl.BlockSpec((1, c, width), lambda b, ci: (b, nc - 1 - ci, 0))
    return pl.pallas_call(
        functools.partial(_wkv_kernel, n_pairs=n_pairs),
        grid=(bsz, nc),
        in_specs=[fwd, bwd] * 6 + [_resident(tri.shape), _resident(m_strict.shape),
                                   _resident(m_incl.shape)],
        out_specs=[fwd, bwd],
        out_shape=[jax.ShapeDtypeStruct((bsz, s, width), F32)] * 2,
        scratch_shapes=[pltpu.VMEM((2 * n_pairs, PAIR, PAIR), F32)],
        compiler_params=_cparams(("parallel", "arbitrary")),
        name="wkv_scan",
    )(r, r, v, v, kk, kk, kd[0], kd[1], bb[0], bb[1], lw[0], lw[1], tri, m_strict, m_incl)


def _rwkv_post_kernel(y0_ref, y1_ref, g_ref, bonus_ref, lw_ref, lb_ref, ones_ref, o_ref):
    ones_bd = ones_ref[...]
    y = y0_ref[0] + y1_ref[0]
    inv_n = 1.0 / HEAD_DIM
    yc = y - _head_sum(y, ones_bd) * inv_n
    var = _head_sum(yc * yc, ones_bd) * inv_n
    yn = yc * lax.rsqrt(var + LNX_EPS) * lw_ref[...] + lb_ref[...]
    o_ref[0] = ((yn + bonus_ref[0]) * g_ref[0]).astype(o_ref.dtype)


def _rwkv_post(y0, y1, g, bonus, lnx_w, lnx_b, ones_bd, tm=512):
    bsz, s, width = y0.shape
    row = pl.BlockSpec((1, tm, width), lambda b, i: (b, i, 0))
    vecs = pl.BlockSpec((1, width), lambda b, i: (0, 0))
    return pl.pallas_call(
        _rwkv_post_kernel,
        grid=(bsz, s // tm),
        in_specs=[row, row, row, row, vecs, vecs, _resident(ones_bd.shape)],
        out_specs=row,
        out_shape=jax.ShapeDtypeStruct((bsz, s, width), F32),
        compiler_params=_cparams(("parallel", "parallel")),
        name="rwkv_post",
    )(y0, y1, g, bonus, lnx_w.reshape(1, width), lnx_b.reshape(1, width), ones_bd)


def _rope_rows(x, cos, sin_signed):
    half = HEAD_DIM // 2
    lane = lax.broadcasted_iota(jnp.int32, (1, PAIR), 1)
    first_half = (lane % HEAD_DIM) < half
    partner = jnp.where(first_half, pltpu.roll(x, PAIR - half, axis=1), pltpu.roll(x, half, axis=1))
    return x * cos + partner * sin_signed


def _dilated_kernel(q_ref, k_ref, v_ref, cos_ref, sin_ref, o_ref,
                    qr, kp, vp, m_s, den_s, num_s, *, seq, pad, rope_rows):
    scale = HEAD_DIM ** -0.5
    zeros_pad = jnp.zeros((pad, PAIR), F32)
    kp[0:pad] = zeros_pad
    vp[0:pad] = zeros_pad
    kp[pad + seq:pad + seq + pad] = zeros_pad
    vp[pad + seq:pad + seq + pad] = zeros_pad

    def rope_body(j, carry):
        rows = pl.ds(pl.multiple_of(j * rope_rows, rope_rows), rope_rows)
        cos = cos_ref[rows, :]
        sin = sin_ref[rows, :]
        qr[rows, :] = _rope_rows(q_ref[0, rows, :], cos, sin) * scale
        dst = pl.ds(pl.multiple_of(pad + j * rope_rows, rope_rows), rope_rows)
        kp[dst, :] = _rope_rows(k_ref[0, rows, :], cos, sin)
        vp[dst, :] = v_ref[0, rows, :]
        return carry

    lax.fori_loop(0, seq // rope_rows, rope_body, 0)

    lane = lax.broadcasted_iota(jnp.int32, (1, PAIR), 1)
    head0 = lane < HEAD_DIM
    qi = lax.broadcasted_iota(jnp.int32, (QBLK, 1), 0)
    kj = lax.broadcasted_iota(jnp.int32, (1, KBLK), 1)
    half_w = QBLK // 2
    band = jnp.abs(kj - half_w - qi) <= half_w

    for branch, (window, dil) in enumerate(DILATED_PAIRS):
        assert window // (2 * dil) == half_w
        m_len = seq // dil
        blocks_per_res = m_len // QBLK

        def block_body(it, carry, dil=dil, m_len=m_len, blocks_per_res=blocks_per_res,
                       first=(branch == 0)):
            res = it // blocks_per_res
            m0 = (it % blocks_per_res) * QBLK
            q_start = res + dil * m0
            k_start = pad + res + dil * (m0 - half_w)
            if dil == 1:
                q_rows = pl.ds(pl.multiple_of(q_start, QBLK), QBLK)
                k_rows = pl.ds(pl.multiple_of(k_start, half_w), KBLK)
            else:
                q_rows = pl.ds(q_start, QBLK, stride=dil)
                k_rows = pl.ds(k_start, KBLK, stride=dil)
            qb = qr[q_rows, :]
            kb = kp[k_rows, :].astype(BF16)
            vb = vp[k_rows, :].astype(BF16)
            kpos = m0 - half_w + kj
            valid = band & (kpos >= 0) & (kpos < m_len)
            heads = []
            for h in range(2):
                qh = jnp.where(head0 if h == 0 else ~head0, qb, 0.0)
                sc = jnp.where(valid, _mm_nt(qh, kb), NEG_BIG)
                mh = jnp.max(sc, axis=-1, keepdims=True)
                ph = jnp.exp(sc - mh)
                dh = jnp.sum(ph, axis=-1, keepdims=True)
                heads.append((mh, dh, _mm(ph, vb)))
            m_b = jnp.where(head0, heads[0][0], heads[1][0])
            d_b = jnp.where(head0, heads[0][1], heads[1][1])
            n_b = jnp.where(head0, heads[0][2], heads[1][2])
            if first:
                m_s[q_rows, :] = m_b
                den_s[q_rows, :] = d_b
                num_s[q_rows, :] = n_b
            else:
                m_old = m_s[q_rows, :]
                m_new = jnp.maximum(m_old, m_b)
                alpha = jnp.exp(m_old - m_new)
                beta = jnp.exp(m_b - m_new)
                m_s[q_rows, :] = m_new
                den_s[q_rows, :] = alpha * den_s[q_rows, :] + beta * d_b
                num_s[q_rows, :] = alpha * num_s[q_rows, :] + beta * n_b
            return carry

        lax.fori_loop(0, seq // QBLK, block_body, 0)

    def out_body(j, carry):
        rows = pl.ds(pl.multiple_of(j * rope_rows, rope_rows), rope_rows)
        o_ref[0, rows, :] = (num_s[rows, :] / den_s[rows, :]).astype(o_ref.dtype)
        return carry

    lax.fori_loop(0, seq // rope_rows, out_body, 0)


def _dilated_attention(z, col0, width):
    bsz, s, _ = z.shape
    n_pairs = width // PAIR
    max_dil = max(d for _, d in DILATED_PAIRS)
    assert s % (QBLK * max_dil) == 0 and col0 % PAIR == 0
    pad = (QBLK // 2) * max_dil
    half = HEAD_DIM // 2
    inv = ROPE_THETA ** (-jnp.arange(half, dtype=F32) / half)
    ang = jnp.arange(s, dtype=F32)[:, None] * inv[None, :]
    cos = jnp.tile(jnp.cos(ang), (1, PAIR // half))
    sin = jnp.tile(jnp.concatenate([-jnp.sin(ang), jnp.sin(ang)], axis=1), (1, 2))
    cb = col0 // PAIR
    col = lambda off: pl.BlockSpec((1, s, PAIR), lambda b, p: (b, 0, cb + off * n_pairs + p))
    rope_rows = 512
    return pl.pallas_call(
        functools.partial(_dilated_kernel, seq=s, pad=pad, rope_rows=rope_rows),
        grid=(bsz, n_pairs),
        in_specs=[col(0), col(1), col(2), _resident((s, PAIR)), _resident((s, PAIR))],
        out_specs=pl.BlockSpec((1, s, PAIR), lambda b, p: (b, 0, p)),
        out_shape=jax.ShapeDtypeStruct((bsz, s, width), F32),
        scratch_shapes=[pltpu.VMEM((s, PAIR), F32),
                        pltpu.VMEM((s + 2 * pad, PAIR), F32),
                        pltpu.VMEM((s + 2 * pad, PAIR), F32),
                        pltpu.VMEM((s, PAIR), F32),
                        pltpu.VMEM((s, PAIR), F32),
                        pltpu.VMEM((s, PAIR), F32)],
        compiler_params=_cparams(("parallel", "parallel")),
        name="dilated_attention",
    )(z, z, z, cos, sin)


def _seq_dft_kernel(a_ref, x_ref, o_ref, acc_ref):
    kk = pl.program_id(3)

    @pl.when(kk == 0)
    def _():
        acc_ref[...] = jnp.zeros_like(acc_ref)

    acc_ref[...] += jnp.dot(a_ref[...], x_ref[0], preferred_element_type=F32)

    @pl.when(kk == pl.num_programs(3) - 1)
    def _():
        o_ref[0] = acc_ref[...].astype(o_ref.dtype)


def _seq_dft(a_cat, x2, tm=1024, tn=1024, tk=512):
    bsz, s, two_d = x2.shape
    d = two_d // 2
    nks = s // tk
    return pl.pallas_call(
        _seq_dft_kernel,
        grid=(bsz, s // tm, d // tn, 2 * nks),
        in_specs=[pl.BlockSpec((tm, tk), lambda b, i, j, kk: (i, kk)),
                  pl.BlockSpec((1, tk, tn),
                               lambda b, i, j, kk: (b, kk % nks, (kk // nks) * (d // tn) + j))],
        out_specs=pl.BlockSpec((1, tm, tn), lambda b, i, j, kk: (b, i, j)),
        out_shape=jax.ShapeDtypeStruct((bsz, s, d), BF16),
        scratch_shapes=[pltpu.VMEM((tm, tn), F32)],
        compiler_params=_cparams(("parallel", "parallel", "parallel", "arbitrary")),
        name="seq_dft",
    )(a_cat, x2)


def _dft_tables(s, d):
    gsz = d // N_FOURIER_GROUPS
    cidx = jnp.arange(gsz, dtype=jnp.int32)
    ang_c = (2.0 * np.pi / gsz) * ((cidx[:, None] * cidx[None, :]) % gsz).astype(F32)
    eye_g = jnp.eye(N_FOURIER_GROUPS, dtype=F32)
    wc = jnp.kron(eye_g, jnp.cos(ang_c)) * (gsz ** -0.5)
    ws = jnp.kron(eye_g, jnp.sin(ang_c)) * (gsz ** -0.5)
    w_ch = jnp.concatenate([wc, ws], axis=1)
    sidx = jnp.arange(s, dtype=jnp.int32)
    ang_s = (2.0 * np.pi / s) * ((sidx[:, None] * sidx[None, :]) % s).astype(F32)
    a_cat = jnp.concatenate([jnp.cos(ang_s), -jnp.sin(ang_s)], axis=1) * (s ** -0.5)
    return w_ch.astype(BF16), a_cat.astype(BF16)


def kernel(x, c, ada_w, ada_b, norm_g, mix_in, mix_conv, rwkv_w0, rwkv_w1, rwkv_w2, rwkv_a0, rwkv_a1, rwkv_a2, rwkv_g1, rwkv_g2, rwkv_k_k, rwkv_k_a, rwkv_r_k, rwkv_lnx_w, rwkv_lnx_b, mix_out, fnet_w, ffn_up, ffn_conv, ffn_down):
    bsz, s, d = x.shape
    depth = ada_w.shape[0]
    width_a = rwkv_k_k.shape[-1]
    width_b = mix_out.shape[1] - width_a
    in_cols_a = 4 * width_a
    ones_bd = jnp.kron(jnp.eye(width_a // HEAD_DIM, dtype=F32),
                       jnp.ones((HEAD_DIM, HEAD_DIM), F32)).astype(BF16)

    mod = _adaln_mod(c, ada_w, ada_b)
    for l in range(depth):
        sh1, sc1, gt1, sh2, sc2, gt2 = [mod[l, :, j * d:(j + 1) * d] for j in range(6)]
        if l % 2 == 0:
            e = l // 2
            z = _in_proj(x, norm_g[l, 0], sc1, sh1, mix_in[e], F32)
            r, v, kk, g, bonus, kd, bb, lw = _rwkv_prep(
                z, mix_conv[e], rwkv_w0[e], rwkv_w1[e], rwkv_w2[e], rwkv_a0[e], rwkv_a1[e],
                rwkv_a2[e], rwkv_g1[e], rwkv_g2[e], rwkv_k_k[e], rwkv_k_a[e], rwkv_r_k[e], ones_bd)
            y0, y1 = _wkv_scan(r, v, kk, kd, bb, lw)
            ya = _rwkv_post(y0, y1, g, bonus, rwkv_lnx_w[e], rwkv_lnx_b[e], ones_bd)
            yb = _dilated_attention(z, in_cols_a, width_b)
            x = _out_proj([ya, yb], [mix_out[e, :width_a], mix_out[e, width_a:]],
                          norm_g[l, 1], gt1, x)
        else:
            w_ch, a_cat = _dft_tables(s, d)
            x2 = _in_proj(x, norm_g[l, 0], sc1, sh1, w_ch, BF16)
            f = _seq_dft(a_cat, x2)
            x = _out_proj([f], [fnet_w[l // 2]], norm_g[l, 1], gt1, x)
        x = _ffn(x, norm_g[l, 2], sc2, sh2, ffn_up[l], ffn_conv[l], ffn_down[l],
                 norm_g[l, 3], gt2)
    return x
```

```python
import functools

import jax
import jax.numpy as jnp
import numpy as np
from jax import lax
from jax.experimental import pallas as pl
from jax.experimental.pallas import tpu as pltpu

F32 = jnp.float32
BF16 = jnp.bfloat16

HEAD_DIM = 64
DILATED_PAIRS = ((128, 1), (512, 4), (2048, 16))
ROPE_THETA = 10000.0
N_FOURIER_GROUPS = 8
RMS_EPS = 1e-6
LNX_EPS = 64e-5
L2_EPS = 1e-12
NEG_BIG = -1e30

LANES = 128
SUBLANES = 8
VMEM_LIMIT_BYTES = 56 * 1024 * 1024

PAIR = 2 * HEAD_DIM
assert PAIR == LANES
HALO = SUBLANES
CHUNK = 64
QBLK = 128
KBLK = 2 * QBLK
ATTN_GROUP = 4


def _cparams(sem):
    return pltpu.CompilerParams(dimension_semantics=sem, vmem_limit_bytes=VMEM_LIMIT_BYTES)


def _resident(shape):
    zeros = (0,) * len(shape)
    return pl.BlockSpec(shape, lambda *_: zeros, pipeline_mode=pl.Buffered(1))


def _mm(a, b):
    return jnp.dot(a.astype(BF16), b.astype(BF16), preferred_element_type=F32)


def _mm_nt(a, b):
    return lax.dot_general(a.astype(BF16), b.astype(BF16), (((1,), (1,)), ((), ())),
                           preferred_element_type=F32)


def _split(x, n):
    parts = []
    rem = x
    for _ in range(n):
        p = rem.astype(BF16)
        parts.append(p)
        rem = rem - p.astype(F32)
    return parts


def _rmsnorm(x, g):
    return x * lax.rsqrt(jnp.mean(x * x, -1, keepdims=True) + RMS_EPS) * g


def _conv3_rows(z, w, n_rows):
    total = z.shape[0]
    y = pltpu.roll(z, 1, axis=0) * w[0:1] + z * w[1:2] + pltpu.roll(z, total - 1, axis=0) * w[2:3]
    return y[HALO:HALO + n_rows]


def _halo_specs(tm, width, seq, col_block=0):
    per = tm // HALO
    last = seq // HALO - 1

    def prev_map(b, i):
        return (b, jnp.maximum(i * per - 1, 0), col_block)

    def next_map(b, i):
        return (b, jnp.minimum((i + 1) * per, last), col_block)

    return [pl.BlockSpec((1, HALO, width), prev_map),
            pl.BlockSpec((1, tm, width), lambda b, i: (b, i, col_block)),
            pl.BlockSpec((1, HALO, width), next_map)]


def _mod_kernel(c_ref, w_ref, b_ref, o_ref):
    c = c_ref[...]
    cs = c * jax.nn.sigmoid(c)
    o_ref[0] = jnp.dot(cs, w_ref[0], precision=lax.Precision.HIGHEST,
                       preferred_element_type=F32) + b_ref[0]


def _adaln_mod(c, ada_w, ada_b):
    depth, d, n = ada_w.shape
    b = c.shape[0]
    tn = n // 4
    return pl.pallas_call(
        _mod_kernel,
        grid=(depth, n // tn),
        in_specs=[pl.BlockSpec((b, d), lambda l, j: (0, 0)),
                  pl.BlockSpec((1, d, tn), lambda l, j: (l, 0, j)),
                  pl.BlockSpec((1, 1, tn), lambda l, j: (l, 0, j))],
        out_specs=pl.BlockSpec((1, b, tn), lambda l, j: (l, 0, j)),
        out_shape=jax.ShapeDtypeStruct((depth, b, n), F32),
        compiler_params=_cparams(("parallel", "parallel")),
        name="adaln_mod",
    )(c, ada_w, ada_b.reshape(depth, 1, n))


def _in_proj_kernel(x_ref, g_ref, sc_ref, sh_ref, w_ref, o_ref):
    h = _rmsnorm(x_ref[0], g_ref[...]) * (1.0 + sc_ref[0]) + sh_ref[0]
    o_ref[0] = _mm(h, w_ref[...]).astype(o_ref.dtype)


def _in_proj(x, g, sc, sh, w, out_dtype, tm=256):
    bsz, s, d = x.shape
    n = w.shape[1]
    vec = pl.BlockSpec((1, 1, d), lambda b, i: (b, 0, 0))
    return pl.pallas_call(
        _in_proj_kernel,
        grid=(bsz, s // tm),
        in_specs=[pl.BlockSpec((1, tm, d), lambda b, i: (b, i, 0)),
                  pl.BlockSpec((1, d), lambda b, i: (0, 0)), vec, vec,
                  _resident((d, n))],
        out_specs=pl.BlockSpec((1, tm, n), lambda b, i: (b, i, 0)),
        out_shape=jax.ShapeDtypeStruct((bsz, s, n), out_dtype),
        compiler_params=_cparams(("parallel", "parallel")),
        name="in_proj",
    )(x, g.reshape(1, d), sc.reshape(bsz, 1, d), sh.reshape(bsz, 1, d), w.astype(BF16))


def _out_proj_kernel(*refs, n_in):
    y_refs = refs[:n_in]
    w_refs = refs[n_in:2 * n_in]
    g_ref, gt_ref, x_ref, o_ref = refs[2 * n_in:]
    acc = _mm(y_refs[0][0], w_refs[0][...])
    for y_ref, w_ref in zip(y_refs[1:], w_refs[1:]):
        acc = acc + _mm(y_ref[0], w_ref[...])
    o_ref[0] = x_ref[0] + gt_ref[0] * _rmsnorm(acc, g_ref[...])


def _out_proj(ys, ws, g, gt, x, tm=512):
    bsz, s, d = x.shape
    n_in = len(ys)
    row = lambda width: pl.BlockSpec((1, tm, width), lambda b, i: (b, i, 0))
    return pl.pallas_call(
        functools.partial(_out_proj_kernel, n_in=n_in),
        grid=(bsz, s // tm),
        in_specs=([row(y.shape[-1]) for y in ys] + [_resident(w.shape) for w in ws]
                  + [pl.BlockSpec((1, d), lambda b, i: (0, 0)),
                     pl.BlockSpec((1, 1, d), lambda b, i: (b, 0, 0)), row(d)]),
        out_specs=row(d),
        out_shape=jax.ShapeDtypeStruct((bsz, s, d), F32),
        compiler_params=_cparams(("parallel", "parallel")),
        name="out_proj",
    )(*ys, *[w.astype(BF16) for w in ws], g.reshape(1, d), gt.reshape(bsz, 1, d), x)


def _ffn_kernel(xp_ref, x_ref, xn_ref, g0_ref, sc_ref, sh_ref, up_ref, cw_ref, dn_ref,
                g1_ref, gt_ref, o_ref, *, tm, d_ff, n_chunks):
    i = pl.program_id(1)
    last = pl.num_programs(1) - 1
    x = x_ref[0]
    xh = jnp.concatenate([xp_ref[0], x, xn_ref[0]], axis=0)
    h = _rmsnorm(xh, g0_ref[...]) * (1.0 + sc_ref[0]) + sh_ref[0]
    rows = lax.broadcasted_iota(jnp.int32, (tm + 2 * HALO, 1), 0)
    lo = jnp.where(i > 0, 0, HALO)
    hi = jnp.where(i < last, tm + 2 * HALO, tm + HALO)
    h = jnp.where((rows >= lo) & (rows < hi), h, 0.0).astype(BF16)
    fc = d_ff // n_chunks
    acc = jnp.zeros((tm, x.shape[1]), F32)
    for f in range(n_chunks):
        gsl = slice(f * fc, (f + 1) * fc)
        vsl = slice(d_ff + f * fc, d_ff + (f + 1) * fc)
        zg = _conv3_rows(_mm(h, up_ref[:, gsl]), cw_ref[:, gsl], tm)
        zv = _conv3_rows(_mm(h, up_ref[:, vsl]), cw_ref[:, vsl], tm)
        act = jax.nn.gelu(zg, approximate=True) * zv
        acc = acc + _mm(act, dn_ref[gsl, :])
    o_ref[0] = x + gt_ref[0] * _rmsnorm(acc, g1_ref[...])


def _ffn(x, g0, sc, sh, up, conv, down, g1, gt, tm=512, n_chunks=2):
    bsz, s, d = x.shape
    d_ff = down.shape[0]
    vec = pl.BlockSpec((1, 1, d), lambda b, i: (b, 0, 0))
    gvec = pl.BlockSpec((1, d), lambda b, i: (0, 0))
    return pl.pallas_call(
        functools.partial(_ffn_kernel, tm=tm, d_ff=d_ff, n_chunks=n_chunks),
        grid=(bsz, s // tm),
        in_specs=(_halo_specs(tm, d, s)
                  + [gvec, vec, vec, _resident(up.shape), _resident(conv.shape),
                     _resident(down.shape), gvec, vec]),
        out_specs=pl.BlockSpec((1, tm, d), lambda b, i: (b, i, 0)),
        out_shape=jax.ShapeDtypeStruct((bsz, s, d), F32),
        compiler_params=_cparams(("parallel", "parallel")),
        name="conv_glu_ffn",
    )(x, x, x, g0.reshape(1, d), sc.reshape(bsz, 1, d), sh.reshape(bsz, 1, d),
      up.astype(BF16), conv, down.astype(BF16), g1.reshape(1, d), gt.reshape(bsz, 1, d))


def _head_sum(x, ones_bd):
    hi, lo = _split(x, 2)
    return (jnp.dot(hi, ones_bd, preferred_element_type=F32)
            + jnp.dot(lo, ones_bd, preferred_element_type=F32))


def _rwkv_prep_kernel(zp_ref, z_ref, zn_ref, cw_ref, w1_ref, w2_ref, vec_ref, ones_ref,
                      r_o, v_o, kk_o, g_o, bonus_o, kd_o, b_o, lw_o, *, tm, width):
    i = pl.program_id(1)
    last = pl.num_programs(1) - 1
    zp = jnp.where(i > 0, zp_ref[0], 0.0)
    zn = jnp.where(i < last, zn_ref[0], 0.0)
    z = jnp.concatenate([zp, z_ref[0], zn], axis=0)
    za = _conv3_rows(z, cw_ref[...], tm)
    r = za[:, 0:width]
    k = za[:, width:2 * width]
    v = za[:, 2 * width:3 * width]
    u = za[:, 3 * width:4 * width]
    ones_bd = ones_ref[...]
    vec = vec_ref[...]
    k_k, k_a, r_k = vec[4:5], vec[5:6], vec[6:7]

    lora = _mm(u, w1_ref[...])
    g_o[0] = _mm(jax.nn.sigmoid(lora[:, 4 * LANES:5 * LANES]), w2_ref[4])
    kk = k * k_k
    kk = kk * lax.rsqrt(_head_sum(kk * kk, ones_bd) + L2_EPS)
    r_o[0] = r
    v_o[0] = v
    kk_o[0] = kk
    bonus = jnp.zeros_like(r)
    for d in range(2):
        q = vec[d:d + 1] + _mm(jnp.tanh(lora[:, d * LANES:(d + 1) * LANES]), w2_ref[d])
        w_log = jnp.minimum(q, 0.0) - jnp.log1p(jnp.exp(-jnp.abs(q))) - 0.5
        a =jax.nn.sigmoid(vec[2 + d:3 + d]
                           + _mm(lora[:, (2 + d) * LANES:(3 + d) * LANES], w2_ref[2 + d]))
        kd = k * (1.0 + (a - 1.0) * k_a)
        kd_o[d, 0] = kd
        b_o[d, 0] = kk * a
        lw_o[d, 0] = -jnp.exp(w_log)
        bonus = bonus + _head_sum(r * kd * r_k, ones_bd)
    bonus_o[0] = bonus * v


def _rwkv_prep(z, conv, w0, w1, w2, a0, a1, a2, g1, g2, k_k, k_a, r_k, ones_bd, tm=256):
    bsz, s, _ = z.shape
    width = k_k.shape[0]
    lora = w1.shape[-1]
    pad_c = lambda m: jnp.pad(m, ((0, 0), (0, LANES - m.shape[1])))
    pad_r = lambda m: jnp.pad(m, ((0, LANES - m.shape[0]), (0, 0)))
    assert lora <= LANES and g1.shape[1] == LANES
    w1cat = jnp.concatenate([pad_c(w1[0]), pad_c(w1[1]), pad_c(a1[0]), pad_c(a1[1]), g1],
                            axis=1).astype(BF16)
    w2cat = jnp.stack([pad_r(w2[0]), pad_r(w2[1]), pad_r(a2[0]), pad_r(a2[1]), g2]).astype(BF16)
    vec = jnp.stack([w0[0], w0[1], a0[0], a0[1], k_k, k_a, r_k.reshape(-1),
                     jnp.zeros_like(k_k)])
    out1 = jax.ShapeDtypeStruct((bsz, s, width), F32)
    out2 = jax.ShapeDtypeStruct((2, bsz, s, width), F32)
    spec1 = pl.BlockSpec((1, tm, width), lambda b, i: (b, i, 0))
    spec2 = pl.BlockSpec((2, 1, tm, width), lambda b, i: (0, b, i, 0))
    return pl.pallas_call(
        functools.partial(_rwkv_prep_kernel, tm=tm, width=width),
        grid=(bsz, s // tm),
        in_specs=(_halo_specs(tm, 4 * width, s)
                  + [_resident(conv.shape), _resident(w1cat.shape), _resident(w2cat.shape),
                     _resident(vec.shape), _resident(ones_bd.shape)]),
        out_specs=[spec1] * 5 + [spec2] * 3,
        out_shape=[out1] * 5 + [out2] * 3,
        compiler_params=_cparams(("parallel", "parallel")),
        name="rwkv_prep",
    )(z, z, z, conv, w1cat, w2cat, vec, ones_bd)


_BNN = (((2,), (1,)), ((0,), (0,)))
_BNT = (((2,), (2,)), ((0,), (0,)))
_BTN = (((1,), (1,)), ((0,), (0,)))


def _bmm(a, b, dims=_BNN):
    return lax.dot_general(a.astype(BF16), b.astype(BF16), dims, preferred_element_type=F32)


def _stack(x, head0):
    return jnp.concatenate([jnp.where(head0, x, 0.0), jnp.where(head0, 0.0, x)], axis=1)


def _wkv_group(r, k, v, kk, b, lw, h0, tri, m_strict, m_incl):
    grp, c, _ = r.shape
    n2 = 2 * c
    lw3 = _split(lw, 3)
    cum = sum(lax.dot_general(tri, p, _BNN, preferred_element_type=F32) for p in lw3)
    tot = jnp.sum(lw, axis=1, keepdims=True)
    ones_c = jnp.ones((grp, c, PAIR), BF16)
    tot_col = sum(lax.dot_general(p, ones_c, _BTN, preferred_element_type=F32) for p in lw3)
    g_inv = jnp.exp(-cum)
    g_rat = jnp.exp(tot - cum)
    head0 = lax.broadcasted_iota(jnp.int32, (1, 1, PAIR), 2) < HEAD_DIM
    kks = _stack(kk * jnp.exp(cum - lw), head0)
    rs = _stack(r * jnp.exp(cum), head0)
    khs = _stack(k * g_inv, head0)
    bhs = _stack(b * g_inv, head0)
    kgs = _stack(k * g_rat, head0)
    bgs = _stack(b * g_rat, head0)
    vs = _stack(v, head0)

    a_all = _bmm(jnp.concatenate([kks, rs], axis=1), jnp.concatenate([khs, bhs], axis=1), _BNT)
    akk = jnp.where(m_strict, a_all[:, :n2, :n2], 0.0)
    akb = jnp.where(m_strict, a_all[:, :n2, n2:], 0.0)
    ark = jnp.where(m_incl, a_all[:, n2:, :n2], 0.0)
    arb = jnp.where(m_incl, a_all[:, n2:, n2:], 0.0)

    eye = (lax.broadcasted_iota(jnp.int32, (1, n2, n2), 1)
           == lax.broadcasted_iota(jnp.int32, (1, n2, n2), 2)).astype(F32)
    n = -akb
    t = eye + n
    p = _bmm(n, n)
    for _ in range(int(np.ceil(np.log2(c))) - 2):
        pt = _bmm(jnp.concatenate([p, t], axis=1), p)
        p = pt[:, :n2]
        t = t + pt[:, n2:]
    t = t + _bmm(t, p)
    ia_t = lax.dot_general((eye + akb).astype(BF16), jnp.concatenate(_split(t, 2), axis=2),
                           _BNN, preferred_element_type=F32)
    t = t + _bmm(t, eye - ia_t[:, :, :n2] - ia_t[:, :, n2:])

    hk = _bmm(jnp.concatenate([kks, rs], axis=1), h0)
    av = _bmm(jnp.concatenate([akk, ark], axis=1), vs)
    us = _bmm(t, hk[:, :n2] + av[:, :n2])
    ys = hk[:, n2:] + av[:, n2:] - _bmm(arb, us)
    y = ys[:, :c] + ys[:, c:]
    h_new = jnp.exp(tot_col) * h0 + _bmm(jnp.concatenate([kgs, -bgs], axis=1),
                                         jnp.concatenate([vs, us], axis=1), _BTN)
    return y, h_new


def _wkv_kernel(r0_ref, r1_ref, v0_ref, v1_ref, kk0_ref, kk1_ref, k0_ref, k1_ref, b0_ref, b1_ref,
                lw0_ref, lw1_ref, tri_ref, ms_ref, mi_ref, y0_ref, y1_ref, h_ref, *, n_pairs):
    @pl.when(pl.program_id(1) == 0)
    def _():
        h_ref[...] = jnp.zeros_like(h_ref)

    lanes = [slice(p * PAIR, (p + 1) * PAIR) for p in range(n_pairs)]

    def group(ref0, ref1):
        return jnp.stack([ref0[:, sl] for sl in lanes] + [ref1[:, sl] for sl in lanes])

    per_dir = lambda ref: jnp.stack([ref[0]] * n_pairs + [ref[1]] * n_pairs)
    y, h_new = _wkv_group(group(r0_ref, r1_ref), group(k0_ref, k1_ref), group(v0_ref, v1_ref),
                          group(kk0_ref, kk1_ref), group(b0_ref, b1_ref), group(lw0_ref, lw1_ref),
                          h_ref[...], per_dir(tri_ref), per_dir(ms_ref) > 0.5, per_dir(mi_ref) > 0.5)
    h_ref[...] = h_new
    for p, sl in enumerate(lanes):
        y0_ref[:, sl] = y[p]
        y1_ref[:, sl] = y[n_pairs + p]


def _wkv_scan(r, v, kk, kd, bb, lw):
    bsz, s, width = r.shape
    c = CHUNK
    nc = s // c
    n_pairs = width // PAIR
    t_idx = np.arange(c)
    before = np.stack([t_idx[None, :] < t_idx[:, None], t_idx[None, :] > t_idx[:, None]])
    eye = np.eye(c, dtype=bool)[None]
    blockdiag = lambda m: np.stack([np.kron(np.eye(2, dtype=bool), m[d]) for d in range(2)])
    tri = jnp.asarray(before | eye, BF16)
    m_strict = jnp.asarray(blockdiag(before), F32)
    m_incl = jnp.asarray(blockdiag(before | eye), F32)

    fwd = pl.BlockSpec((None, c, width), lambda b, ci: (b, ci, 0))
    bwd = pl.BlockSpec((None, c, width), lambda b, ci: (b, nc - 1 - ci, 0))
    fwd_d = pl.BlockSpec((None, None, c, width), lambda b, ci: (0, b, ci, 0))
    bwd_d = pl.BlockSpec((None, None, c, width), lambda b, ci: (1, b, nc - 1 - ci, 0))
    return pl.pallas_call(
        functools.partial(_wkv_kernel, n_pairs=n_pairs),
        grid=(bsz, nc),
        in_specs=[fwd, bwd] * 3 + [fwd_d, bwd_d] * 3 + [
            _resident(tri.shape), _resident(m_strict.shape), _resident(m_incl.shape)],
        out_specs=[fwd, bwd],
        out_shape=[jax.ShapeDtypeStruct((bsz, s, width), F32)] * 2,
        scratch_shapes=[pltpu.VMEM((2 * n_pairs, PAIR, PAIR), F32)],
        compiler_params=_cparams(("parallel", "arbitrary")),
        name="wkv_scan",
    )(r, r, v, v, kk, kk, kd, kd, bb, bb, lw, lw, tri, m_strict, m_incl)


def _rwkv_post_kernel(y0_ref, y1_ref, g_ref, bonus_ref, lw_ref, lb_ref, ones_ref, o_ref):
    ones_bd = ones_ref[...]
    y = y0_ref[0] + y1_ref[0]
    inv_n = 1.0 / HEAD_DIM
    yc = y - _head_sum(y, ones_bd) * inv_n
    var = _head_sum(yc * yc, ones_bd) * inv_n
    yn = yc * lax.rsqrt(var + LNX_EPS) * lw_ref[...] + lb_ref[...]
    o_ref[0] = ((yn + bonus_ref[0]) * g_ref[0]).astype(o_ref.dtype)


def _rwkv_post(y0, y1, g, bonus, lnx_w, lnx_b, ones_bd, tm=512):
    bsz, s, width = y0.shape
    row = pl.BlockSpec((1, tm, width), lambda b, i: (b, i, 0))
    vecs = pl.BlockSpec((1, width), lambda b, i: (0, 0))
    return pl.pallas_call(
        _rwkv_post_kernel,
        grid=(bsz, s // tm),
        in_specs=[row, row, row, row, vecs, vecs, _resident(ones_bd.shape)],
        out_specs=row,
        out_shape=jax.ShapeDtypeStruct((bsz, s, width), F32),
        compiler_params=_cparams(("parallel", "parallel")),
        name="rwkv_post",
    )(y0, y1, g, bonus, lnx_w.reshape(1, width), lnx_b.reshape(1, width), ones_bd)


def _rope_rows(x, cos, sin_signed):
    half = HEAD_DIM // 2
    lane = lax.broadcasted_iota(jnp.int32, (1, PAIR), 1)
    first_half = (lane % HEAD_DIM) < half
    partner = jnp.where(first_half, pltpu.roll(x, PAIR - half, axis=1), pltpu.roll(x, half, axis=1))
    return x * cos + partner * sin_signed


def _dilated_kernel(q_ref, k_ref, v_ref, cos_ref, sin_ref, o_ref,
                    qr, kp, vp, m_s, den_s, num_s, *, seq, pad, rope_rows):
    scale = HEAD_DIM ** -0.5
    zeros_pad = jnp.zeros((pad, PAIR), F32)
    kp[0:pad] = zeros_pad
    vp[0:pad] = zeros_pad
    kp[pad + seq:pad + seq + pad] = zeros_pad
    vp[pad + seq:pad + seq + pad] = zeros_pad

    def rope_body(j, carry):
        rows = pl.ds(pl.multiple_of(j * rope_rows, rope_rows), rope_rows)
        cos = cos_ref[rows, :]
        sin = sin_ref[rows, :]
        qr[rows, :] = _rope_rows(q_ref[0, rows, :], cos, sin) * scale
        dst = pl.ds(pl.multiple_of(pad + j * rope_rows, rope_rows), rope_rows)
        kp[dst, :] = _rope_rows(k_ref[0, rows, :], cos, sin)
        vp[dst, :] = v_ref[0, rows, :]
        return carry

    lax.fori_loop(0, seq // rope_rows, rope_body, 0)

    lane = lax.broadcasted_iota(jnp.int32, (1, PAIR), 1)
    head0 = lane < HEAD_DIM
    qi = lax.broadcasted_iota(jnp.int32, (2 * QBLK, 1), 0) % QBLK
    kj = lax.broadcasted_iota(jnp.int32, (1, KBLK), 1)
    half_w = QBLK // 2
    band = jnp.abs(kj - half_w - qi) <= half_w
    ones_v = jnp.ones((ATTN_GROUP, KBLK, PAIR), BF16)

    for branch, (window, dil) in enumerate(DILATED_PAIRS):
        assert window // (2 * dil) == half_w
        m_len = seq // dil
        blocks_per_res = m_len // QBLK

        def group_body(it, carry, dil=dil, m_len=m_len, blocks_per_res=blocks_per_res,
                       first=(branch == 0)):
            q_rows, qs, ks, vs, in_range = [], [], [], [], []
            for g in range(ATTN_GROUP):
                blk = it * ATTN_GROUP + g
                res = blk // blocks_per_res
                m0 = (blk % blocks_per_res) * QBLK
                q_start = res + dil * m0
                k_start = pad + res + dil * (m0 - half_w)
                if dil == 1:
                    rows = pl.ds(pl.multiple_of(q_start, QBLK), QBLK)
                    k_rows = pl.ds(pl.multiple_of(k_start, half_w), KBLK)
                else:
                    rows = pl.ds(q_start, QBLK, stride=dil)
                    k_rows = pl.ds(k_start, KBLK, stride=dil)
                qb = qr[rows, :]
                q_rows.append(rows)
                qs.append(jnp.concatenate([jnp.where(head0, qb, 0.0), jnp.where(head0, 0.0, qb)],
                                          axis=0).astype(BF16))
                ks.append(kp[k_rows, :].astype(BF16))
                vs.append(vp[k_rows, :].astype(BF16))
                kpos = m0 - half_w + kj
                in_range.append((kpos >= 0) & (kpos < m_len))
            kb = jnp.stack(ks)
            vb = jnp.stack(vs)
            sc = _bmm(jnp.stack(qs), kb, _BNT)
            sc = jnp.stack([jnp.where(in_range[g], jnp.where(band, sc[g], NEG_BIG), NEG_BIG)
                            for g in range(ATTN_GROUP)])
            mx = jnp.max(sc, axis=-1, keepdims=True)
            p = jnp.exp(sc - mx).astype(BF16)
            num = lax.dot_general(p, vb, _BNN, preferred_element_type=F32)
            den = lax.dot_general(p, ones_v, _BNN, preferred_element_type=F32)
            m_b = jnp.where(head0, mx[:, :QBLK], mx[:, QBLK:])
            d_b = jnp.where(head0, den[:, :QBLK], den[:, QBLK:])
            n_b = jnp.where(head0, num[:, :QBLK], num[:, QBLK:])
            for g, rows in enumerate(q_rows):
                if first:
                    m_s[rows, :] = m_b[g]
                    den_s[rows, :] = d_b[g]
                    num_s[rows, :] = n_b[g]
                else:
                    m_old = m_s[rows, :]
                    m_new = jnp.maximum(m_old, m_b[g])
                    alpha = jnp.exp(m_old - m_new)
                    beta = jnp.exp(m_b[g] - m_new)
                    m_s[rows, :] = m_new
                    den_s[rows, :] = alpha * den_s[rows, :] + beta * d_b[g]
                    num_s[rows, :] = alpha * num_s[rows, :] + beta * n_b[g]
            return carry

        lax.fori_loop(0, seq // (QBLK * ATTN_GROUP), group_body, 0)

    def out_body(j, carry):
        rows = pl.ds(pl.multiple_of(j * rope_rows, rope_rows), rope_rows)
        o_ref[0, rows, :] = (num_s[rows, :] / den_s[rows, :]).astype(o_ref.dtype)
        return carry

    lax.fori_loop(0, seq // rope_rows, out_body, 0)


def _dilated_attention(z, col0, width):
    bsz, s, _ = z.shape
    n_pairs = width // PAIR
    max_dil = max(d for _, d in DILATED_PAIRS)
    assert s % (QBLK * max_dil) == 0 and s % (QBLK * ATTN_GROUP) == 0 and col0 % PAIR == 0
    pad = (QBLK // 2) * max_dil
    half = HEAD_DIM // 2
    inv = ROPE_THETA ** (-jnp.arange(half, dtype=F32) / half)
    ang = jnp.arange(s, dtype=F32)[:, None] * inv[None, :]
    cos = jnp.tile(jnp.cos(ang), (1, PAIR // half))
    sin = jnp.tile(jnp.concatenate([-jnp.sin(ang), jnp.sin(ang)], axis=1), (1, 2))
    cb = col0 // PAIR
    col = lambda off: pl.BlockSpec((1, s, PAIR), lambda b, p: (b, 0, cb + off * n_pairs + p))
    rope_rows = 512
    return pl.pallas_call(
        functools.partial(_dilated_kernel, seq=s, pad=pad, rope_rows=rope_rows),
        grid=(bsz, n_pairs),
        in_specs=[col(0), col(1), col(2), _resident((s, PAIR)), _resident((s, PAIR))],
        out_specs=pl.BlockSpec((1, s, PAIR), lambda b, p: (b, 0, p)),
        out_shape=jax.ShapeDtypeStruct((bsz, s, width), F32),
        scratch_shapes=[pltpu.VMEM((s, PAIR), F32),
                        pltpu.VMEM((s + 2 * pad, PAIR), F32),
                        pltpu.VMEM((s + 2 * pad, PAIR), F32),
                        pltpu.VMEM((s, PAIR), F32),
                        pltpu.VMEM((s, PAIR), F32),
                        pltpu.VMEM((s, PAIR), F32)],
        compiler_params=_cparams(("parallel", "parallel")),
        name="dilated_attention",
    )(z, z, z, cos, sin)


def _seq_dft_kernel(xc_ref, xs_ref, cm_ref, sm_ref, jm_ref, alt_ref, o_ref, xe_s, yo_s,
                    *, seq, t):
    h = seq // 2
    nb = h // t
    scale = seq ** -0.5
    jm = jm_ref[...]
    row0 = lax.broadcasted_iota(jnp.int32, (t, 1), 0) == 0
    for i in range(nb):
        rows = slice(i * t, (i + 1) * t)
        mirror = slice(seq - (i + 1) * t, seq - i * t)
        pc = jnp.dot(jm, xc_ref[0, mirror, :], preferred_element_type=F32)
        ps = jnp.dot(jm, xs_ref[0, mirror, :], preferred_element_type=F32)
        if i > 0:
            first = slice(seq - i * t, seq - i * t + 1)
            pc = jnp.where(row0, xc_ref[0, first, :].astype(F32), pc)
            ps = jnp.where(row0, xs_ref[0, first, :].astype(F32), ps)
        xe_s[rows, :] = (xc_ref[0, rows, :].astype(F32) + pc).astype(BF16)
        yo_s[rows, :] = (xs_ref[0, rows, :].astype(F32) - ps).astype(BF16)
    xe = xe_s[...]
    p = jnp.dot(cm_ref[...], xe, preferred_element_type=F32)
    q = jnp.dot(sm_ref[...], yo_s[...], preferred_element_type=F32)
    alt = alt_ref[...]
    x_h = xc_ref[0, h:h + 1, :].astype(F32) * scale
    k_odd = lax.broadcasted_iota(jnp.int32, (h, 1), 0) % 2 == 1
    alt_x = jnp.where(k_odd, -x_h, x_h)
    o_ref[0, 0:h, :] = (p - q + alt_x).astype(o_ref.dtype)
    pq = (p + q + alt_x).astype(BF16)
    f_h = scale * jnp.dot(alt, xe, preferred_element_type=F32)[0:1] + x_h
    for i in range(nb):
        rev = jnp.dot(jm, pq[h - (i + 1) * t:h - i * t], preferred_element_type=F32)
        first = f_h if i == 0 else pq[h - i * t:h - i * t + 1].astype(F32)
        o_ref[0, h + i * t:h + (i + 1) * t, :] = jnp.where(row0, first, rev).astype(o_ref.dtype)


def _seq_dft(cm, sm, x2, tn=256, t=128):
    bsz, s, two_d = x2.shape
    d = two_d // 2
    h = s // 2
    assert h % 2 == 0 and h % t == 0 and d % tn == 0
    r = np.arange(t)
    jm = jnp.asarray((r[None, :] == t - r[:, None]) & (r[:, None] >= 1), BF16)
    alt = jnp.asarray(np.where(np.arange(8)[:, None] == 0, 1.0 - 2.0 * (np.arange(h) % 2), 0.0), BF16)
    nj = d // tn
    return pl.pallas_call(
        functools.partial(_seq_dft_kernel, seq=s, t=t),
        grid=(bsz, nj),
        in_specs=[pl.BlockSpec((1, s, tn), lambda b, j: (b, 0, j)),
                  pl.BlockSpec((1, s, tn), lambda b, j: (b, 0, nj + j)),
                  _resident((h, h)), _resident((h, h)), _resident((t, t)), _resident((8, h))],
        out_specs=pl.BlockSpec((1, s, tn), lambda b, j: (b, 0, j)),
        out_shape=jax.ShapeDtypeStruct((bsz, s, d), BF16),
        scratch_shapes=[pltpu.VMEM((h, tn), BF16), pltpu.VMEM((h, tn), BF16)],
        compiler_params=_cparams(("parallel", "parallel")),
        name="seq_dft",
    )(x2, x2, cm, sm, jm, alt)


def _dft_tables(s, d):
    gsz = d // N_FOURIER_GROUPS
    cidx = jnp.arange(gsz, dtype=jnp.int32)
    ang_c = (2.0 * np.pi / gsz) * ((cidx[:, None] * cidx[None, :]) % gsz).astype(F32)
    eye_g = jnp.eye(N_FOURIER_GROUPS, dtype=F32)
    wc = jnp.kron(eye_g, jnp.cos(ang_c)) * (gsz ** -0.5)
    ws = jnp.kron(eye_g, jnp.sin(ang_c)) * (gsz ** -0.5)
    w_ch = jnp.concatenate([wc, ws], axis=1)
    q = 64
    h = s // 2
    assert h % q == 0
    sidx = jnp.arange(h, dtype=jnp.int32)[None, :]
    ang = lambda kvals: (2.0 * np.pi / s) * ((kvals[:, None] * sidx) % s).astype(F32)
    ang_hi = ang(q * jnp.arange(h // q, dtype=jnp.int32))[:, None, :]
    ang_lo = ang(jnp.arange(q, dtype=jnp.int32))[None, :, :]
    scale = s ** -0.5
    ch, sh, cl, sl = jnp.cos(ang_hi) * scale, jnp.sin(ang_hi) * scale, jnp.cos(ang_lo), jnp.sin(ang_lo)
    cm = (ch * cl - sh * sl).reshape(h, h).astype(BF16)
    sm = (sh * cl + ch * sl).reshape(h, h).astype(BF16)
    return w_ch.astype(BF16), cm, sm


def kernel(x, c, ada_w, ada_b, norm_g, mix_in, mix_conv, rwkv_w0, rwkv_w1, rwkv_w2, rwkv_a0, rwkv_a1, rwkv_a2, rwkv_g1, rwkv_g2, rwkv_k_k, rwkv_k_a, rwkv_r_k, rwkv_lnx_w, rwkv_lnx_b, mix_out, fnet_w, ffn_up, ffn_conv, ffn_down):
    bsz, s, d = x.shape
    depth = ada_w.shape[0]
    width_a = rwkv_k_k.shape[-1]
    width_b = mix_out.shape[1] - width_a
    in_cols_a = 4 * width_a
    ones_bd = jnp.kron(jnp.eye(width_a // HEAD_DIM, dtype=F32),
                       jnp.ones((HEAD_DIM, HEAD_DIM), F32)).astype(BF16)

    mod = _adaln_mod(c, ada_w, ada_b)
    for l in range(depth):
        sh1, sc1, gt1, sh2, sc2, gt2 = [mod[l, :, j * d:(j + 1) * d] for j in range(6)]
        if l % 2 == 0:
            e = l // 2
            z = _in_proj(x, norm_g[l, 0], sc1, sh1, mix_in[e], F32)
            r, v, kk, g, bonus, kd, bb, lw = _rwkv_prep(
                z, mix_conv[e], rwkv_w0[e], rwkv_w1[e], rwkv_w2[e], rwkv_a0[e], rwkv_a1[e],
                rwkv_a2[e], rwkv_g1[e], rwkv_g2[e], rwkv_k_k[e], rwkv_k_a[e], rwkv_r_k[e], ones_bd)
            y0, y1 = _wkv_scan(r, v, kk, kd, bb, lw)
            ya = _rwkv_post(y0, y1, g, bonus, rwkv_lnx_w[e], rwkv_lnx_b[e], ones_bd)
            yb = _dilated_attention(z, in_cols_a, width_b)
            x = _out_proj([ya, yb], [mix_out[e, :width_a], mix_out[e, width_a:]],
                          norm_g[l, 1], gt1, x)
        else:
            w_ch, cm, sm = _dft_tables(s, d)
            x2 = _in_proj(x, norm_g[l, 0], sc1, sh1, w_ch, BF16)
            f = _seq_dft(cm, sm, x2)
            x = _out_proj([f], [fnet_w[l // 2]], norm_g[l, 1], gt1, x)
        x = _ffn(x, norm_g[l, 2], sc2, sh2, ffn_up[l], ffn_conv[l], ffn_down[l],
                 norm_g[l, 3], gt2)
    return x
```

```python
import functools

import jax
import jax.numpy as jnp
import numpy as np
from jax import lax
from jax.experimental import pallas as pl
from jax.experimental.pallas import tpu as pltpu

F32 = jnp.float32
BF16 = jnp.bfloat16

HEAD_DIM = 64
DILATED_PAIRS = ((128, 1), (512, 4), (2048, 16))
ROPE_THETA = 10000.0
N_FOURIER_GROUPS = 8
RMS_EPS = 1e-6
LNX_EPS = 64e-5
L2_EPS = 1e-12
NEG_BIG = -1e30

LANES = 128
SUBLANES = 8
VMEM_LIMIT_BYTES = 56 * 1024 * 1024

PAIR = 2 * HEAD_DIM
assert PAIR == LANES
HALO = SUBLANES
CHUNK = 64
QBLK = 128
KBLK = 2 * QBLK
ATTN_GROUP = 4


def _cparams(sem):
    return pltpu.CompilerParams(dimension_semantics=sem, vmem_limit_bytes=VMEM_LIMIT_BYTES)


def _resident(shape):
    zeros = (0,) * len(shape)
    return pl.BlockSpec(shape, lambda *_: zeros, pipeline_mode=pl.Buffered(1))


def _mm(a, b):
    return jnp.dot(a.astype(BF16), b.astype(BF16), preferred_element_type=F32)


def _mm_nt(a, b):
    return lax.dot_general(a.astype(BF16), b.astype(BF16), (((1,), (1,)), ((), ())),
                           preferred_element_type=F32)


def _split(x, n):
    parts = []
    rem = x
    for _ in range(n):
        p = rem.astype(BF16)
        parts.append(p)
        rem = rem - p.astype(F32)
    return parts


def _rmsnorm(x, g):
    return x * lax.rsqrt(jnp.mean(x * x, -1, keepdims=True) + RMS_EPS) * g


def _conv3_rows(z, w, n_rows):
    total = z.shape[0]
    y = pltpu.roll(z, 1, axis=0) * w[0:1] + z * w[1:2] + pltpu.roll(z, total - 1, axis=0) * w[2:3]
    return y[HALO:HALO + n_rows]


def _halo_specs(tm, width, seq, col_block=0):
    per = tm // HALO
    last = seq // HALO - 1

    def prev_map(b, i):
        return (b, jnp.maximum(i * per - 1, 0), col_block)

    def next_map(b, i):
        return (b, jnp.minimum((i + 1) * per, last), col_block)

    return [pl.BlockSpec((1, HALO, width), prev_map),
            pl.BlockSpec((1, tm, width), lambda b, i: (b, i, col_block)),
            pl.BlockSpec((1, HALO, width), next_map)]


def _mod_kernel(c_ref, w_ref, b_ref, o_ref):
    c = c_ref[...]
    cs = c * jax.nn.sigmoid(c)
    o_ref[0] = jnp.dot(cs, w_ref[0], precision=lax.Precision.HIGHEST,
                       preferred_element_type=F32) + b_ref[0]


def _adaln_mod(c, ada_w, ada_b):
    depth, d, n = ada_w.shape
    b = c.shape[0]
    tn = n // 4
    return pl.pallas_call(
        _mod_kernel,
        grid=(depth, n // tn),
        in_specs=[pl.BlockSpec((b, d), lambda l, j: (0, 0)),
                  pl.BlockSpec((1, d, tn), lambda l, j: (l, 0, j)),
                  pl.BlockSpec((1, 1, tn), lambda l, j: (l, 0, j))],
        out_specs=pl.BlockSpec((1, b, tn), lambda l, j: (l, 0, j)),
        out_shape=jax.ShapeDtypeStruct((depth, b, n), F32),
        compiler_params=_cparams(("parallel", "parallel")),
        name="adaln_mod",
    )(c, ada_w, ada_b.reshape(depth, 1, n))


def _in_proj_kernel(x_ref, g_ref, sc_ref, sh_ref, w_ref, o_ref):
    h = _rmsnorm(x_ref[0], g_ref[...]) * (1.0 + sc_ref[0]) + sh_ref[0]
    o_ref[0] = _mm(h, w_ref[...]).astype(o_ref.dtype)


def _in_proj(x, g, sc, sh, w, out_dtype, tm=256):
    bsz, s, d = x.shape
    n = w.shape[1]
    vec = pl.BlockSpec((1, 1, d), lambda b, i: (b, 0, 0))
    return pl.pallas_call(
        _in_proj_kernel,
        grid=(bsz, s // tm),
        in_specs=[pl.BlockSpec((1, tm, d), lambda b, i: (b, i, 0)),
                  pl.BlockSpec((1, d), lambda b, i: (0, 0)), vec, vec,
                  _resident((d, n))],
        out_specs=pl.BlockSpec((1, tm, n), lambda b, i: (b, i, 0)),
        out_shape=jax.ShapeDtypeStruct((bsz, s, n), out_dtype),
        compiler_params=_cparams(("parallel", "parallel")),
        name="in_proj",
    )(x, g.reshape(1, d), sc.reshape(bsz, 1, d), sh.reshape(bsz, 1, d), w.astype(BF16))


def _out_proj_kernel(*refs, n_in):
    y_refs = refs[:n_in]
    w_refs = refs[n_in:2 * n_in]
    g_ref, gt_ref, x_ref, o_ref = refs[2 * n_in:]
    acc = _mm(y_refs[0][0], w_refs[0][...])
    for y_ref, w_ref in zip(y_refs[1:], w_refs[1:]):
        acc = acc + _mm(y_ref[0], w_ref[...])
    o_ref[0] = x_ref[0] + gt_ref[0] * _rmsnorm(acc, g_ref[...])


def _out_proj(ys, ws, g, gt, x, tm=512):
    bsz, s, d = x.shape
    n_in = len(ys)
    row = lambda width: pl.BlockSpec((1, tm, width), lambda b, i: (b, i, 0))
    return pl.pallas_call(
        functools.partial(_out_proj_kernel, n_in=n_in),
        grid=(bsz, s // tm),
        in_specs=([row(y.shape[-1]) for y in ys] + [_resident(w.shape) for w in ws]
                  + [pl.BlockSpec((1, d), lambda b, i: (0, 0)),
                     pl.BlockSpec((1, 1, d), lambda b, i: (b, 0, 0)), row(d)]),
        out_specs=row(d),
        out_shape=jax.ShapeDtypeStruct((bsz, s, d), F32),
        compiler_params=_cparams(("parallel", "parallel")),
        name="out_proj",
    )(*ys, *[w.astype(BF16) for w in ws], g.reshape(1, d), gt.reshape(bsz, 1, d), x)


def _ffn_kernel(xp_ref, x_ref, xn_ref, g0_ref, sc_ref, sh_ref, up_ref, cw_ref, dn_ref,
                g1_ref, gt_ref, o_ref, *, tm, d_ff, n_chunks):
    i = pl.program_id(1)
    last = pl.num_programs(1) - 1
    x = x_ref[0]
    xh = jnp.concatenate([xp_ref[0], x, xn_ref[0]], axis=0)
    h = _rmsnorm(xh, g0_ref[...]) * (1.0 + sc_ref[0]) + sh_ref[0]
    rows = lax.broadcasted_iota(jnp.int32, (tm + 2 * HALO, 1), 0)
    lo = jnp.where(i > 0, 0, HALO)
    hi = jnp.where(i < last, tm + 2 * HALO, tm + HALO)
    h = jnp.where((rows >= lo) & (rows < hi), h, 0.0).astype(BF16)
    fc = d_ff // n_chunks
    acc = jnp.zeros((tm, x.shape[1]), F32)
    for f in range(n_chunks):
        gsl = slice(f * fc, (f + 1) * fc)
        vsl = slice(d_ff + f * fc, d_ff + (f + 1) * fc)
        zg = _conv3_rows(_mm(h, up_ref[:, gsl]), cw_ref[:, gsl], tm)
        zv = _conv3_rows(_mm(h, up_ref[:, vsl]), cw_ref[:, vsl], tm)
        act = jax.nn.gelu(zg, approximate=True) * zv
        acc = acc + _mm(act, dn_ref[gsl, :])
    o_ref[0] = x + gt_ref[0] * _rmsnorm(acc, g1_ref[...])


def _ffn(x, g0, sc, sh, up, conv, down, g1, gt, tm=512, n_chunks=2):
    bsz, s, d = x.shape
    d_ff = down.shape[0]
    vec = pl.BlockSpec((1, 1, d), lambda b, i: (b, 0, 0))
    gvec = pl.BlockSpec((1, d), lambda b, i: (0, 0))
    return pl.pallas_call(
        functools.partial(_ffn_kernel, tm=tm, d_ff=d_ff, n_chunks=n_chunks),
        grid=(bsz, s // tm),
        in_specs=(_halo_specs(tm, d, s)
                  + [gvec, vec, vec, _resident(up.shape), _resident(conv.shape),
                     _resident(down.shape), gvec, vec]),
        out_specs=pl.BlockSpec((1, tm, d), lambda b, i: (b, i, 0)),
        out_shape=jax.ShapeDtypeStruct((bsz, s, d), F32),
        compiler_params=_cparams(("parallel", "parallel")),
        name="conv_glu_ffn",
    )(x, x, x, g0.reshape(1, d), sc.reshape(bsz, 1, d), sh.reshape(bsz, 1, d),
      up.astype(BF16), conv, down.astype(BF16), g1.reshape(1, d), gt.reshape(bsz, 1, d))


def _head_sum(x, ones_bd):
    hi, lo = _split(x, 2)
    return (jnp.dot(hi, ones_bd, preferred_element_type=F32)
            + jnp.dot(lo, ones_bd, preferred_element_type=F32))


def _mix_prep_kernel(xp_ref, x_ref, xn_ref, g0_ref, sc_ref, sh_ref, win_ref, cw_ref, w1_ref,
                     w2_ref, vec_ref, ones_ref, zb_o, r_o, v_o, kk_o, g_o, bonus_o, kd_o, b_o,
                     lw_o, *, tm, width):
    i = pl.program_id(1)
    last = pl.num_programs(1) - 1
    xh = jnp.concatenate([xp_ref[0], x_ref[0], xn_ref[0]], axis=0)
    h = _rmsnorm(xh, g0_ref[...]) * (1.0 + sc_ref[0]) + sh_ref[0]
    rows = lax.broadcasted_iota(jnp.int32, (tm + 2 * HALO, 1), 0)
    lo = jnp.where(i > 0, 0, HALO)
    hi = jnp.where(i < last, tm + 2 * HALO, tm + HALO)
    h = jnp.where((rows >= lo) & (rows < hi), h, 0.0).astype(BF16)
    cols_a = 4 * width
    za = _conv3_rows(_mm(h, win_ref[:, :cols_a]), cw_ref[...], tm)
    r = za[:, 0:width]
    k = za[:, width:2 * width]
    v = za[:, 2 * width:3 * width]
    u = za[:, 3 * width:4 * width]
    ones_bd = ones_ref[...]
    vec = vec_ref[...]
    k_k, k_a, r_k = vec[4:5], vec[5:6], vec[6:7]

    lora = _mm(u, w1_ref[...])
    g_o[0] = _mm(jax.nn.sigmoid(lora[:, 4 * LANES:5 * LANES]), w2_ref[4]).astype(g_o.dtype)
    kk = k * k_k
    kk = kk * lax.rsqrt(_head_sum(kk * kk, ones_bd) + L2_EPS)
    r_o[0] = r.astype(r_o.dtype)
    v_o[0] = v.astype(v_o.dtype)
    kk_o[0] = kk.astype(kk_o.dtype)
    bonus = jnp.zeros_like(r)
    for d in range(2):
        q = vec[d:d + 1] + _mm(jnp.tanh(lora[:, d * LANES:(d + 1) * LANES]), w2_ref[d])
        w_log = jnp.minimum(q, 0.0) - jnp.log1p(jnp.exp(-jnp.abs(q))) - 0.5
        a = jax.nn.sigmoid(vec[2 + d:3 + d]
                           + _mm(lora[:, (2 + d) * LANES:(3 + d) * LANES], w2_ref[2 + d]))
        kd = k * (1.0 + (a - 1.0) * k_a)
        kd_o[d, 0] = kd.astype(kd_o.dtype)
        b_o[d, 0] = (kk * a).astype(b_o.dtype)
        lw_o[d, 0] = -jnp.exp(w_log)
        bonus = bonus + _head_sum(r * kd * r_k, ones_bd)
    bonus_o[0] = (bonus * v).astype(bonus_o.dtype)
    zb_o[0] = _mm(h[HALO:HALO + tm], win_ref[:, cols_a:]).astype(zb_o.dtype)


def _mix_prep(x, g0, sc, sh, w_in, conv, w0, w1, w2, a0, a1, a2, g1, g2, k_k, k_a, r_k, ones_bd,
              tm=256):
    bsz, s, d = x.shape
    width = k_k.shape[0]
    lora = w1.shape[-1]
    cols_b = w_in.shape[1] - 4 * width
    pad_c = lambda m: jnp.pad(m, ((0, 0), (0, LANES - m.shape[1])))
    pad_r = lambda m: jnp.pad(m, ((0, LANES - m.shape[0]), (0, 0)))
    assert lora <= LANES and g1.shape[1] == LANES
    w1cat = jnp.concatenate([pad_c(w1[0]), pad_c(w1[1]), pad_c(a1[0]), pad_c(a1[1]), g1],
                            axis=1).astype(BF16)
    w2cat = jnp.stack([pad_r(w2[0]), pad_r(w2[1]), pad_r(a2[0]), pad_r(a2[1]), g2]).astype(BF16)
    vec = jnp.stack([w0[0], w0[1], a0[0], a0[1], k_k, k_a, r_k.reshape(-1),
                     jnp.zeros_like(k_k)])
    out1 = jax.ShapeDtypeStruct((bsz, s, width), BF16)
    out2 = jax.ShapeDtypeStruct((2, bsz, s, width), BF16)
    spec1 = pl.BlockSpec((1, tm, width), lambda b, i: (b, i, 0))
    spec2 = pl.BlockSpec((2, 1, tm, width), lambda b, i: (0, b, i, 0))
    mvec = pl.BlockSpec((1, 1, d), lambda b, i: (b, 0, 0))
    return pl.pallas_call(
        functools.partial(_mix_prep_kernel, tm=tm, width=width),
        grid=(bsz, s // tm),
        in_specs=(_halo_specs(tm, d, s)
                  + [pl.BlockSpec((1, d), lambda b, i: (0, 0)), mvec, mvec,
                     _resident(w_in.shape), _resident(conv.shape), _resident(w1cat.shape),
                     _resident(w2cat.shape), _resident(vec.shape), _resident(ones_bd.shape)]),
        out_specs=([pl.BlockSpec((1, tm, cols_b), lambda b, i: (b, i, 0))]
                   + [spec1] * 5 + [spec2] * 3),
        out_shape=([jax.ShapeDtypeStruct((bsz, s, cols_b), F32)] + [out1] * 5 + [out2] * 2
                   + [jax.ShapeDtypeStruct((2, bsz, s, width), F32)]),
        compiler_params=_cparams(("parallel", "parallel")),
        name="mix_in_rwkv_prep",
    )(x, x, x, g0.reshape(1, d), sc.reshape(bsz, 1, d), sh.reshape(bsz, 1, d),
      w_in.astype(BF16), conv, w1cat, w2cat, vec, ones_bd)


_BNN = (((2,), (1,)), ((0,), (0,)))
_BNT = (((2,), (2,)), ((0,), (0,)))
_BTN = (((1,), (1,)), ((0,), (0,)))


def _bmm(a, b, dims=_BNN):
    return lax.dot_general(a.astype(BF16), b.astype(BF16), dims, preferred_element_type=F32)


def _stack(x, head0):
    return jnp.concatenate([jnp.where(head0, x, 0.0), jnp.where(head0, 0.0, x)], axis=1)


def _wkv_group(r, k, v, kk, b, lw, h0, tri, m_strict, m_incl):
    grp, c, _ = r.shape
    n2 = 2 * c
    lw3 = _split(lw, 3)
    cum = sum(lax.dot_general(tri, p, _BNN, preferred_element_type=F32) for p in lw3)
    tot = jnp.sum(lw, axis=1, keepdims=True)
    ones_c = jnp.ones((grp, c, PAIR), BF16)
    tot_col = sum(lax.dot_general(p, ones_c, _BTN, preferred_element_type=F32) for p in lw3)
    g_inv = jnp.exp(-cum)
    g_rat = jnp.exp(tot - cum)
    lane = lax.broadcasted_iota(jnp.int32, (1, 1, PAIR), 2)
    head0 = lane < HEAD_DIM
    left = lane < c
    kt = kk * jnp.exp(cum - lw)
    rt = r * jnp.exp(cum)
    vs = _stack(v, head0)

    a_all = _bmm(jnp.concatenate([kt, rt], axis=1),
                 jnp.concatenate([_stack(k * g_inv, head0), _stack(b * g_inv, head0)], axis=1),
                 _BNT)
    akk = jnp.where(m_strict, a_all[:, :c, :n2], 0.0)
    akb = jnp.where(m_strict, a_all[:, :c, n2:], 0.0)
    ark = jnp.where(m_incl, a_all[:, c:, :n2], 0.0)
    arb = jnp.where(m_incl, a_all[:, c:, n2:], 0.0)

    eye = (lax.broadcasted_iota(jnp.int32, (1, c, n2), 1)
           == lax.broadcasted_iota(jnp.int32, (1, c, n2), 2) % c).astype(F32)
    n = -akb
    t = eye + n
    p = _bmm(n, _stack(n, left))
    for _ in range(int(np.ceil(np.log2(c))) - 2):
        pt = _bmm(jnp.concatenate([p, t], axis=1), _stack(p, left))
        p = pt[:, :c]
        t = t + pt[:, c:]
    t = t + _bmm(t, _stack(p, left))
    t_hi, t_lo = _split(t, 2)
    ia_t = lax.dot_general((eye + akb).astype(BF16),
                           jnp.concatenate([_stack(t_hi, left), _stack(t_lo, left)], axis=2),
                           _BNN, preferred_element_type=F32)
    t = t + _bmm(t, _stack(eye - ia_t[:, :, :n2] - ia_t[:, :, n2:], left))

    hk = _bmm(jnp.concatenate([kt, rt], axis=1), h0)
    av = _bmm(jnp.concatenate([akk, ark], axis=1), vs)
    u = _bmm(t, _stack(hk[:, :c] + av[:, :c], head0))
    y = hk[:, c:] + av[:, c:] - _bmm(arb, _stack(u, head0))
    same_head = ((lax.broadcasted_iota(jnp.int32, (1, PAIR, PAIR), 1) < HEAD_DIM)
                 == (lax.broadcasted_iota(jnp.int32, (1, PAIR, PAIR), 2) < HEAD_DIM))
    upd = _bmm(jnp.concatenate([k * g_rat, -(b * g_rat)], axis=1),
               jnp.concatenate([v, u], axis=1), _BTN)
    h_new = jnp.exp(tot_col) * h0 + jnp.where(same_head, upd, 0.0)
    return y, h_new


def _wkv_kernel(r0_ref, r1_ref, v0_ref, v1_ref, kk0_ref, kk1_ref, k0_ref, k1_ref, b0_ref, b1_ref,
                lw0_ref, lw1_ref, tri_ref, ms_ref, mi_ref, y0_ref, y1_ref, h_ref, *, n_pairs, nb):
    @pl.when(pl.program_id(1) == 0)
    def _():
        h_ref[...] = jnp.zeros_like(h_ref)

    cells = [(bi, slice(p * PAIR, (p + 1) * PAIR)) for bi in range(nb) for p in range(n_pairs)]

    def group(ref0, ref1):
        return jnp.stack([ref0[bi, :, sl] for bi, sl in cells]
                         + [ref1[bi, :, sl] for bi, sl in cells]).astype(F32)

    per_dir = lambda ref: jnp.stack([ref[0]] * len(cells) + [ref[1]] * len(cells))
    y, h_new = _wkv_group(group(r0_ref, r1_ref), group(k0_ref, k1_ref), group(v0_ref, v1_ref),
                          group(kk0_ref, kk1_ref), group(b0_ref, b1_ref), group(lw0_ref, lw1_ref),
                          h_ref[...], per_dir(tri_ref), per_dir(ms_ref) > 0.5, per_dir(mi_ref) > 0.5)
    h_ref[...] = h_new
    for i, (bi, sl) in enumerate(cells):
        y0_ref[bi, :, sl] = y[i]
        y1_ref[bi, :, sl] = y[len(cells) + i]


def _wkv_scan(r, v, kk, kd, bb, lw):
    bsz, s, width = r.shape
    c = CHUNK
    nc = s // c
    n_pairs = width // PAIR
    t_idx = np.arange(c)
    before = np.stack([t_idx[None, :] < t_idx[:, None], t_idx[None, :] > t_idx[:, None]])
    eye = np.eye(c, dtype=bool)[None]
    tri = jnp.asarray(before | eye, BF16)
    m_strict = jnp.asarray(np.tile(before, (1, 1, 2)), F32)
    m_incl = jnp.asarray(np.tile(before | eye, (1, 1, 2)), F32)

    nb = 2 if bsz % 2 == 0 else 1
    fwd = pl.BlockSpec((nb, c, width), lambda b, ci: (b, ci, 0))
    bwd = pl.BlockSpec((nb, c, width), lambda b, ci: (b, nc - 1 - ci, 0))
    fwd_d = pl.BlockSpec((None, nb, c, width), lambda b, ci: (0, b, ci, 0))
    bwd_d = pl.BlockSpec((None, nb, c, width), lambda b, ci: (1, b, nc - 1 - ci, 0))
    return pl.pallas_call(
        functools.partial(_wkv_kernel, n_pairs=n_pairs, nb=nb),
        grid=(bsz // nb, nc),
        in_specs=[fwd, bwd] * 3 + [fwd_d, bwd_d] * 3 + [
            _resident(tri.shape), _resident(m_strict.shape), _resident(m_incl.shape)],
        out_specs=[fwd, bwd],
        out_shape=[jax.ShapeDtypeStruct((bsz, s, width), F32)] * 2,
        scratch_shapes=[pltpu.VMEM((2 * nb * n_pairs, PAIR, PAIR), F32)],
        compiler_params=_cparams(("parallel", "arbitrary")),
        name="wkv_scan",
    )(r, r, v, v, kk, kk, kd, kd, bb, bb, lw, lw, tri, m_strict, m_incl)


def _rwkv_post_kernel(y0_ref, y1_ref, g_ref, bonus_ref, lw_ref, lb_ref, ones_ref, o_ref):
    ones_bd = ones_ref[...]
    y = y0_ref[0] + y1_ref[0]
    inv_n = 1.0 / HEAD_DIM
    yc = y - _head_sum(y, ones_bd) * inv_n
    var = _head_sum(yc * yc, ones_bd) * inv_n
    yn = yc * lax.rsqrt(var + LNX_EPS) * lw_ref[...] + lb_ref[...]
    o_ref[0] = ((yn + bonus_ref[0].astype(F32)) * g_ref[0].astype(F32)).astype(o_ref.dtype)


def _rwkv_post(y0, y1, g, bonus, lnx_w, lnx_b, ones_bd, tm=512):
    bsz, s, width = y0.shape
    row = pl.BlockSpec((1, tm, width), lambda b, i: (b, i, 0))
    vecs = pl.BlockSpec((1, width), lambda b, i: (0, 0))
    return pl.pallas_call(
        _rwkv_post_kernel,
        grid=(bsz, s // tm),
        in_specs=[row, row, row, row, vecs, vecs, _resident(ones_bd.shape)],
        out_specs=row,
        out_shape=jax.ShapeDtypeStruct((bsz, s, width), F32),
        compiler_params=_cparams(("parallel", "parallel")),
        name="rwkv_post",
    )(y0, y1, g, bonus, lnx_w.reshape(1, width), lnx_b.reshape(1, width), ones_bd)


def _rope_rows(x, cos, sin_signed):
    half = HEAD_DIM // 2
    lane = lax.broadcasted_iota(jnp.int32, (1, PAIR), 1)
    first_half = (lane % HEAD_DIM) < half
    partner = jnp.where(first_half, pltpu.roll(x, PAIR - half, axis=1), pltpu.roll(x, half, axis=1))
    return x * cos + partner * sin_signed


def _dilated_kernel(q_ref, k_ref, v_ref, cos_ref, sin_ref, o_ref,
                    qr, kp, vp, m_s, den_s, num_s, *, seq, pad, rope_rows):
    scale = HEAD_DIM ** -0.5
    zeros_pad = jnp.zeros((pad, PAIR), F32)
    kp[0:pad] = zeros_pad
    vp[0:pad] = zeros_pad
    kp[pad + seq:pad + seq + pad] = zeros_pad
    vp[pad + seq:pad + seq + pad] = zeros_pad

    def rope_body(j, carry):
        rows = pl.ds(pl.multiple_of(j * rope_rows, rope_rows), rope_rows)
        cos = cos_ref[rows, :]
        sin = sin_ref[rows, :]
        qr[rows, :] = _rope_rows(q_ref[0, rows, :], cos, sin) * scale
        dst = pl.ds(pl.multiple_of(pad + j * rope_rows, rope_rows), rope_rows)
        kp[dst, :] = _rope_rows(k_ref[0, rows, :], cos, sin)
        vp[dst, :] = v_ref[0, rows, :]
        return carry

    lax.fori_loop(0, seq // rope_rows, rope_body, 0)

    lane = lax.broadcasted_iota(jnp.int32, (1, PAIR), 1)
    head0 = lane < HEAD_DIM
    qi = lax.broadcasted_iota(jnp.int32, (2 * QBLK, 1), 0) % QBLK
    kj = lax.broadcasted_iota(jnp.int32, (1, KBLK), 1)
    half_w = QBLK // 2
    band = jnp.abs(kj - half_w - qi) <= half_w
    ones_v = jnp.ones((ATTN_GROUP, KBLK, PAIR), BF16)

    for branch, (window, dil) in enumerate(DILATED_PAIRS):
        assert window // (2 * dil) == half_w
        m_len = seq // dil
        blocks_per_res = m_len // QBLK

        def group_body(it, carry, dil=dil, m_len=m_len, blocks_per_res=blocks_per_res,
                       first=(branch == 0)):
            q_rows, qs, ks, vs, in_range = [], [], [], [], []
            for g in range(ATTN_GROUP):
                blk = it * ATTN_GROUP + g
                res = blk // blocks_per_res
                m0 = (blk % blocks_per_res) * QBLK
                q_start = res + dil * m0
                k_start = pad + res + dil * (m0 - half_w)
                if dil == 1:
                    rows = pl.ds(pl.multiple_of(q_start, QBLK), QBLK)
                    k_rows = pl.ds(pl.multiple_of(k_start, half_w), KBLK)
                else:
                    rows = pl.ds(q_start, QBLK, stride=dil)
                    k_rows = pl.ds(k_start, KBLK, stride=dil)
                qb = qr[rows, :]
                q_rows.append(rows)
                qs.append(jnp.concatenate([jnp.where(head0, qb, 0.0), jnp.where(head0, 0.0, qb)],
                                          axis=0).astype(BF16))
                ks.append(kp[k_rows, :].astype(BF16))
                vs.append(vp[k_rows, :].astype(BF16))
                kpos = m0 - half_w + kj
                in_range.append((kpos >= 0) & (kpos < m_len))
            kb = jnp.stack(ks)
            vb = jnp.stack(vs)
            sc = _bmm(jnp.stack(qs), kb, _BNT)
            sc = jnp.stack([jnp.where(in_range[g], jnp.where(band, sc[g], NEG_BIG), NEG_BIG)
                            for g in range(ATTN_GROUP)])
            mx = jnp.max(sc, axis=-1, keepdims=True)
            p = jnp.exp(sc - mx).astype(BF16)
            num = lax.dot_general(p, vb, _BNN, preferred_element_type=F32)
            den = lax.dot_general(p, ones_v, _BNN, preferred_element_type=F32)
            m_b = jnp.where(head0, mx[:, :QBLK], mx[:, QBLK:])
            d_b = jnp.where(head0, den[:, :QBLK], den[:, QBLK:])
            n_b = jnp.where(head0, num[:, :QBLK], num[:, QBLK:])
            for g, rows in enumerate(q_rows):
                if first:
                    m_s[rows, :] = m_b[g]
                    den_s[rows, :] = d_b[g]
                    num_s[rows, :] = n_b[g]
                else:
                    m_old = m_s[rows, :]
                    m_new = jnp.maximum(m_old, m_b[g])
                    alpha = jnp.exp(m_old - m_new)
                    beta = jnp.exp(m_b[g] - m_new)
                    m_s[rows, :] = m_new
                    den_s[rows, :] = alpha * den_s[rows, :] + beta * d_b[g]
                    num_s[rows, :] = alpha * num_s[rows, :] + beta * n_b[g]
            return carry

        lax.fori_loop(0, seq // (QBLK * ATTN_GROUP), group_body, 0)

    def out_body(j, carry):
        rows = pl.ds(pl.multiple_of(j * rope_rows, rope_rows), rope_rows)
        o_ref[0, rows, :] = (num_s[rows, :] / den_s[rows, :]).astype(o_ref.dtype)
        return carry

    lax.fori_loop(0, seq // rope_rows, out_body, 0)


def _dilated_attention(z, col0, width):
    bsz, s, _ = z.shape
    n_pairs = width // PAIR
    max_dil = max(d for _, d in DILATED_PAIRS)
    assert s % (QBLK * max_dil) == 0 and s % (QBLK * ATTN_GROUP) == 0 and col0 % PAIR == 0
    pad = (QBLK // 2) * max_dil
    half = HEAD_DIM // 2
    inv = ROPE_THETA ** (-jnp.arange(half, dtype=F32) / half)
    ang = jnp.arange(s, dtype=F32)[:, None] * inv[None, :]
    cos = jnp.tile(jnp.cos(ang), (1, PAIR // half))
    sin = jnp.tile(jnp.concatenate([-jnp.sin(ang), jnp.sin(ang)], axis=1), (1, 2))
    cb = col0 // PAIR
    col = lambda off: pl.BlockSpec((1, s, PAIR), lambda b, p: (b, 0, cb + off * n_pairs + p))
    rope_rows = 512
    return pl.pallas_call(
        functools.partial(_dilated_kernel, seq=s, pad=pad, rope_rows=rope_rows),
        grid=(bsz, n_pairs),
        in_specs=[col(0), col(1), col(2), _resident((s, PAIR)), _resident((s, PAIR))],
        out_specs=pl.BlockSpec((1, s, PAIR), lambda b, p: (b, 0, p)),
        out_shape=jax.ShapeDtypeStruct((bsz, s, width), F32),
        scratch_shapes=[pltpu.VMEM((s, PAIR), F32),
                        pltpu.VMEM((s + 2 * pad, PAIR), F32),
                        pltpu.VMEM((s + 2 * pad, PAIR), F32),
                        pltpu.VMEM((s, PAIR), F32),
                        pltpu.VMEM((s, PAIR), F32),
                        pltpu.VMEM((s, PAIR), F32)],
        compiler_params=_cparams(("parallel", "parallel")),
        name="dilated_attention",
    )(z, z, z, cos, sin)


def _seq_dft_kernel(xc_ref, xs_ref, cm_ref, sm_ref, jm_ref, alt_ref, o_ref, xe_s, yo_s,
                    *, seq, t):
    h = seq // 2
    nb = h // t
    scale = seq ** -0.5
    jm = jm_ref[...]
    row0 = lax.broadcasted_iota(jnp.int32, (t, 1), 0) == 0
    for i in range(nb):
        rows = slice(i * t, (i + 1) * t)
        mirror = slice(seq - (i + 1) * t, seq - i * t)
        pc = jnp.dot(jm, xc_ref[0, mirror, :], preferred_element_type=F32)
        ps = jnp.dot(jm, xs_ref[0, mirror, :], preferred_element_type=F32)
        if i > 0:
            first = slice(seq - i * t, seq - i * t + 1)
            pc = jnp.where(row0, xc_ref[0, first, :].astype(F32), pc)
            ps = jnp.where(row0, xs_ref[0, first, :].astype(F32), ps)
        xe_s[rows, :] = (xc_ref[0, rows, :].astype(F32) + pc).astype(BF16)
        yo_s[rows, :] = (xs_ref[0, rows, :].astype(F32) - ps).astype(BF16)
    xe = xe_s[...]
    p = jnp.dot(cm_ref[...], xe, preferred_element_type=F32)
    q = jnp.dot(sm_ref[...], yo_s[...], preferred_element_type=F32)
    alt = alt_ref[...]
    x_h = xc_ref[0, h:h + 1, :].astype(F32) * scale
    k_odd = lax.broadcasted_iota(jnp.int32, (h, 1), 0) % 2 == 1
    alt_x = jnp.where(k_odd, -x_h, x_h)
    o_ref[0, 0:h, :] = (p - q + alt_x).astype(o_ref.dtype)
    pq = (p + q + alt_x).astype(BF16)
    f_h = scale * jnp.dot(alt, xe, preferred_element_type=F32)[0:1] + x_h
    for i in range(nb):
        rev = jnp.dot(jm, pq[h - (i + 1) * t:h - i * t], preferred_element_type=F32)
        first = f_h if i == 0 else pq[h - i * t:h - i * t + 1].astype(F32)
        o_ref[0, h + i * t:h + (i + 1) * t, :] = jnp.where(row0, first, rev).astype(o_ref.dtype)


def _seq_dft(cm, sm, x2, tn=256, t=128):
    bsz, s, two_d = x2.shape
    d = two_d // 2
    h = s // 2
    assert h % 2 == 0 and h % t == 0 and d % tn == 0
    r = np.arange(t)
    jm = jnp.asarray((r[None, :] == t - r[:, None]) & (r[:, None] >= 1), BF16)
    alt = jnp.asarray(np.where(np.arange(8)[:, None] == 0, 1.0 - 2.0 * (np.arange(h) % 2), 0.0), BF16)
    nj = d // tn
    return pl.pallas_call(
        functools.partial(_seq_dft_kernel, seq=s, t=t),
        grid=(bsz, nj),
        in_specs=[pl.BlockSpec((1, s, tn), lambda b, j: (b, 0, j)),
                  pl.BlockSpec((1, s, tn), lambda b, j: (b, 0, nj + j)),
                  _resident((h, h)), _resident((h, h)), _resident((t, t)), _resident((8, h))],
        out_specs=pl.BlockSpec((1, s, tn), lambda b, j: (b, 0, j)),
        out_shape=jax.ShapeDtypeStruct((bsz, s, d), BF16),
        scratch_shapes=[pltpu.VMEM((h, tn), BF16), pltpu.VMEM((h, tn), BF16)],
        compiler_params=_cparams(("parallel", "parallel")),
        name="seq_dft",
    )(x2, x2, cm, sm, jm, alt)


def _dft_tables(s, d):
    gsz = d // N_FOURIER_GROUPS
    cidx = jnp.arange(gsz, dtype=jnp.int32)
    ang_c = (2.0 * np.pi / gsz) * ((cidx[:, None] * cidx[None, :]) % gsz).astype(F32)
    eye_g = jnp.eye(N_FOURIER_GROUPS, dtype=F32)
    wc = jnp.kron(eye_g, jnp.cos(ang_c)) * (gsz ** -0.5)
    ws = jnp.kron(eye_g, jnp.sin(ang_c)) * (gsz ** -0.5)
    w_ch = jnp.concatenate([wc, ws], axis=1)
    q = 64
    h = s // 2
    assert h % q == 0
    sidx = jnp.arange(h, dtype=jnp.int32)[None, :]
    ang = lambda kvals: (2.0 * np.pi / s) * ((kvals[:, None] * sidx) % s).astype(F32)
    ang_hi = ang(q * jnp.arange(h // q, dtype=jnp.int32))[:, None, :]
    ang_lo = ang(jnp.arange(q, dtype=jnp.int32))[None, :, :]
    scale = s ** -0.5
    ch, sh, cl, sl = jnp.cos(ang_hi) * scale, jnp.sin(ang_hi) * scale, jnp.cos(ang_lo), jnp.sin(ang_lo)
    cm = (ch * cl - sh * sl).reshape(h, h).astype(BF16)
    sm = (sh * cl + ch * sl).reshape(h, h).astype(BF16)
    return w_ch.astype(BF16), cm, sm


def kernel(x, c, ada_w, ada_b, norm_g, mix_in, mix_conv, rwkv_w0, rwkv_w1, rwkv_w2, rwkv_a0, rwkv_a1, rwkv_a2, rwkv_g1, rwkv_g2, rwkv_k_k, rwkv_k_a, rwkv_r_k, rwkv_lnx_w, rwkv_lnx_b, mix_out, fnet_w, ffn_up, ffn_conv, ffn_down):
    bsz, s, d = x.shape
    depth = ada_w.shape[0]
    width_a = rwkv_k_k.shape[-1]
    width_b = mix_out.shape[1] - width_a
    in_cols_a = 4 * width_a
    ones_bd = jnp.kron(jnp.eye(width_a // HEAD_DIM, dtype=F32),
                       jnp.ones((HEAD_DIM, HEAD_DIM), F32)).astype(BF16)

    mod = _adaln_mod(c, ada_w, ada_b)
    for l in range(depth):
        sh1, sc1, gt1, sh2, sc2, gt2 = [mod[l, :, j * d:(j + 1) * d] for j in range(6)]
        if l % 2 == 0:
            e = l // 2
            zb, r, v, kk, g, bonus, kd, bb, lw = _mix_prep(
                x, norm_g[l, 0], sc1, sh1, mix_in[e], mix_conv[e], rwkv_w0[e], rwkv_w1[e],
                rwkv_w2[e], rwkv_a0[e], rwkv_a1[e], rwkv_a2[e], rwkv_g1[e], rwkv_g2[e],
                rwkv_k_k[e], rwkv_k_a[e], rwkv_r_k[e], ones_bd)
            y0, y1 = _wkv_scan(r, v, kk, kd, bb, lw)
            ya = _rwkv_post(y0, y1, g, bonus, rwkv_lnx_w[e], rwkv_lnx_b[e], ones_bd)
            yb = _dilated_attention(zb, 0, width_b)
            x = _out_proj([ya, yb], [mix_out[e, :width_a], mix_out[e, width_a:]],
                          norm_g[l, 1], gt1, x)
        else:
            w_ch, cm, sm = _dft_tables(s, d)
            x2 = _in_proj(x, norm_g[l, 0], sc1, sh1, w_ch, BF16)
            f = _seq_dft(cm, sm, x2)
            x = _out_proj([f], [fnet_w[l // 2]], norm_g[l, 1], gt1, x)
        x = _ffn(x, norm_g[l, 2], sc2, sh2, ffn_up[l], ffn_conv[l], ffn_down[l],
                 norm_g[l, 3], gt2)
    return x
```

```python
import functools

import jax
import jax.numpy as jnp
import numpy as np
from jax import lax
from jax.experimental import pallas as pl
from jax.experimental.pallas import tpu as pltpu

F32 = jnp.float32
BF16 = jnp.bfloat16

HEAD_DIM = 64
DILATED_PAIRS = ((128, 1), (512, 4), (2048, 16))
ROPE_THETA = 10000.0
N_FOURIER_GROUPS = 8
RMS_EPS = 1e-6
LNX_EPS = 64e-5
L2_EPS = 1e-12
NEG_BIG = -1e30

LANES = 128
SUBLANES = 8
VMEM_LIMIT_BYTES = 56 * 1024 * 1024

PAIR = 2 * HEAD_DIM
assert PAIR == LANES
HALO = SUBLANES
CHUNK = 64
QBLK = 128
KBLK = 2 * QBLK
ATTN_GROUP = 4


def _cparams(sem):
    return pltpu.CompilerParams(dimension_semantics=sem, vmem_limit_bytes=VMEM_LIMIT_BYTES)


def _resident(shape):
    zeros = (0,) * len(shape)
    return pl.BlockSpec(shape, lambda *_: zeros, pipeline_mode=pl.Buffered(1))


def _mm(a, b):
    return jnp.dot(a.astype(BF16), b.astype(BF16), preferred_element_type=F32)


def _mm_nt(a, b):
    return lax.dot_general(a.astype(BF16), b.astype(BF16), (((1,), (1,)), ((), ())),
                           preferred_element_type=F32)


def _split(x, n):
    parts = []
    rem = x
    for _ in range(n):
        p = rem.astype(BF16)
        parts.append(p)
        rem = rem - p.astype(F32)
    return parts


def _rmsnorm(x, g):
    return x * lax.rsqrt(jnp.mean(x * x, -1, keepdims=True) + RMS_EPS) * g


def _conv3_rows(z, w, n_rows):
    total = z.shape[0]
    y = pltpu.roll(z, 1, axis=0) * w[0:1] + z * w[1:2] + pltpu.roll(z, total - 1, axis=0) * w[2:3]
    return y[HALO:HALO + n_rows]


def _halo_specs(tm, width, seq, col_block=0):
    per = tm // HALO
    last = seq // HALO - 1

    def prev_map(b, i):
        return (b, jnp.maximum(i * per - 1, 0), col_block)

    def next_map(b, i):
        return (b, jnp.minimum((i + 1) * per, last), col_block)

    return [pl.BlockSpec((1, HALO, width), prev_map),
            pl.BlockSpec((1, tm, width), lambda b, i: (b, i, col_block)),
            pl.BlockSpec((1, HALO, width), next_map)]


def _mod_kernel(c_ref, w_ref, b_ref, o_ref):
    c = c_ref[...]
    cs = c * jax.nn.sigmoid(c)
    o_ref[0] = jnp.dot(cs, w_ref[0], precision=lax.Precision.HIGHEST,
                       preferred_element_type=F32) + b_ref[0]


def _adaln_mod(c, ada_w, ada_b):
    depth, d, n = ada_w.shape
    b = c.shape[0]
    tn = n // 4
    return pl.pallas_call(
        _mod_kernel,
        grid=(depth, n // tn),
        in_specs=[pl.BlockSpec((b, d), lambda l, j: (0, 0)),
                  pl.BlockSpec((1, d, tn), lambda l, j: (l, 0, j)),
                  pl.BlockSpec((1, 1, tn), lambda l, j: (l, 0, j))],
        out_specs=pl.BlockSpec((1, b, tn), lambda l, j: (l, 0, j)),
        out_shape=jax.ShapeDtypeStruct((depth, b, n), F32),
        compiler_params=_cparams(("parallel", "parallel")),
        name="adaln_mod",
    )(c, ada_w, ada_b.reshape(depth, 1, n))


def _chan_dft_kernel(x_ref, g_ref, sc_ref, sh_ref, w_ref, o_ref):
    h = (_rmsnorm(x_ref[0], g_ref[...]) * (1.0 + sc_ref[0]) + sh_ref[0]).astype(BF16)
    d = h.shape[1]
    gsz = w_ref.shape[0]
    w = w_ref[...]
    for grp in range(d // gsz):
        cols = slice(grp * gsz, (grp + 1) * gsz)
        xy = jnp.dot(h[:, cols], w, preferred_element_type=F32)
        o_ref[0, :, cols] = xy[:, :gsz].astype(o_ref.dtype)
        o_ref[0, :, d + grp * gsz:d + (grp + 1) * gsz] = xy[:, gsz:].astype(o_ref.dtype)


def _chan_dft(x, g, sc, sh, w, tm=512):
    bsz, s, d = x.shape
    assert w.shape[0] % LANES == 0 and d % w.shape[0] == 0
    vec = pl.BlockSpec((1, 1, d), lambda b, i: (b, 0, 0))
    return pl.pallas_call(
        _chan_dft_kernel,
        grid=(bsz, s // tm),
        in_specs=[pl.BlockSpec((1, tm, d), lambda b, i: (b, i, 0)),
                  pl.BlockSpec((1, d), lambda b, i: (0, 0)), vec, vec,
                  _resident(w.shape)],
        out_specs=pl.BlockSpec((1, tm, 2 * d), lambda b, i: (b, i, 0)),
        out_shape=jax.ShapeDtypeStruct((bsz, s, 2 * d), BF16),
        compiler_params=_cparams(("parallel", "parallel")),
        name="chan_dft",
    )(x, g.reshape(1, d), sc.reshape(bsz, 1, d), sh.reshape(bsz, 1, d), w)


def _out_proj_kernel(*refs, n_in):
    y_refs = refs[:n_in]
    w_refs = refs[n_in:2 * n_in]
    g_ref, gt_ref, x_ref, o_ref = refs[2 * n_in:]
    acc = _mm(y_refs[0][0], w_refs[0][...])
    for y_ref, w_ref in zip(y_refs[1:], w_refs[1:]):
        acc = acc + _mm(y_ref[0], w_ref[...])
    o_ref[0] = x_ref[0] + gt_ref[0] * _rmsnorm(acc, g_ref[...])


def _out_proj(ys, ws, g, gt, x, tm=512):
    bsz, s, d = x.shape
    n_in = len(ys)
    row = lambda width: pl.BlockSpec((1, tm, width), lambda b, i: (b, i, 0))
    return pl.pallas_call(
        functools.partial(_out_proj_kernel, n_in=n_in),
        grid=(bsz, s // tm),
        in_specs=([row(y.shape[-1]) for y in ys] + [_resident(w.shape) for w in ws]
                  + [pl.BlockSpec((1, d), lambda b, i: (0, 0)),
                     pl.BlockSpec((1, 1, d), lambda b, i: (b, 0, 0)), row(d)]),
        out_specs=row(d),
        out_shape=jax.ShapeDtypeStruct((bsz, s, d), F32),
        compiler_params=_cparams(("parallel", "parallel")),
        name="out_proj",
    )(*ys, *[w.astype(BF16) for w in ws], g.reshape(1, d), gt.reshape(bsz, 1, d), x)


def _ffn_kernel(xp_ref, x_ref, xn_ref, g0_ref, sc_ref, sh_ref, up_ref, cw_ref, dn_ref,
                g1_ref, gt_ref, o_ref, *, tm, d_ff, n_chunks):
    i = pl.program_id(1)
    last = pl.num_programs(1) - 1
    x = x_ref[0]
    xh = jnp.concatenate([xp_ref[0], x, xn_ref[0]], axis=0)
    h = _rmsnorm(xh, g0_ref[...]) * (1.0 + sc_ref[0]) + sh_ref[0]
    rows = lax.broadcasted_iota(jnp.int32, (tm + 2 * HALO, 1), 0)
    lo = jnp.where(i > 0, 0, HALO)
    hi = jnp.where(i < last, tm + 2 * HALO, tm + HALO)
    h = jnp.where((rows >= lo) & (rows < hi), h, 0.0).astype(BF16)
    fc = d_ff // n_chunks
    acc = jnp.zeros((tm, x.shape[1]), F32)
    for f in range(n_chunks):
        gsl = slice(f * fc, (f + 1) * fc)
        vsl = slice(d_ff + f * fc, d_ff + (f + 1) * fc)
        zg = _conv3_rows(_mm(h, up_ref[:, gsl]), cw_ref[:, gsl], tm)
        zv = _conv3_rows(_mm(h, up_ref[:, vsl]), cw_ref[:, vsl], tm)
        act = jax.nn.gelu(zg, approximate=True) * zv
        acc = acc + _mm(act, dn_ref[gsl, :])
    o_ref[0] = x + gt_ref[0] * _rmsnorm(acc, g1_ref[...])


def _ffn(x, g0, sc, sh, up, conv, down, g1, gt, tm=512, n_chunks=2):
    bsz, s, d = x.shape
    d_ff = down.shape[0]
    vec = pl.BlockSpec((1, 1, d), lambda b, i: (b, 0, 0))
    gvec = pl.BlockSpec((1, d), lambda b, i: (0, 0))
    return pl.pallas_call(
        functools.partial(_ffn_kernel, tm=tm, d_ff=d_ff, n_chunks=n_chunks),
        grid=(bsz, s // tm),
        in_specs=(_halo_specs(tm, d, s)
                  + [gvec, vec, vec, _resident(up.shape), _resident(conv.shape),
                     _resident(down.shape), gvec, vec]),
        out_specs=pl.BlockSpec((1, tm, d), lambda b, i: (b, i, 0)),
        out_shape=jax.ShapeDtypeStruct((bsz, s, d), F32),
        compiler_params=_cparams(("parallel", "parallel")),
        name="conv_glu_ffn",
    )(x, x, x, g0.reshape(1, d), sc.reshape(bsz, 1, d), sh.reshape(bsz, 1, d),
      up.astype(BF16), conv, down.astype(BF16), g1.reshape(1, d), gt.reshape(bsz, 1, d))


def _head_sum(x, ones_bd):
    hi, lo = _split(x, 2)
    return (jnp.dot(hi, ones_bd, preferred_element_type=F32)
            + jnp.dot(lo, ones_bd, preferred_element_type=F32))


def _mix_prep_kernel(xp_ref, x_ref, xn_ref, g0_ref, sc_ref, sh_ref, win_ref, cw_ref, w1_ref,
                     w2_ref, vec_ref, ones_ref, zb_o, r_o, v_o, kk_o, g_o, bonus_o, kd_o, b_o,
                     lw_o, *, tm, sub, width):
    i = pl.program_id(1)
    last = pl.num_programs(1) - 1
    xh = jnp.concatenate([xp_ref[0], x_ref[0], xn_ref[0]], axis=0)
    h = _rmsnorm(xh, g0_ref[...]) * (1.0 + sc_ref[0]) + sh_ref[0]
    rows = lax.broadcasted_iota(jnp.int32, (tm + 2 * HALO, 1), 0)
    lo = jnp.where(i > 0, 0, HALO)
    hi = jnp.where(i < last, tm + 2 * HALO, tm + HALO)
    h = jnp.where((rows >= lo) & (rows < hi), h, 0.0).astype(BF16)
    cols_a = 4 * width
    ones_bd = ones_ref[...]
    vec = vec_ref[...]
    k_k, k_a, r_k = vec[4:5], vec[5:6], vec[6:7]
    for j in range(tm // sub):
        out = slice(j * sub, (j + 1) * sub)
        hs = h[j * sub:(j + 1) * sub + 2 * HALO]
        za = _conv3_rows(_mm(hs, win_ref[:, :cols_a]), cw_ref[...], sub)
        r = za[:, 0:width]
        k = za[:, width:2 * width]
        v = za[:, 2 * width:3 * width]
        u = za[:, 3 * width:4 * width]

        lora = _mm(u, w1_ref[...])
        g_o[0, out] = _mm(jax.nn.sigmoid(lora[:, 4 * LANES:5 * LANES]), w2_ref[4]).astype(g_o.dtype)
        kk = k * k_k
        kk = kk * lax.rsqrt(_head_sum(kk * kk, ones_bd) + L2_EPS)
        r_o[0, out] = r.astype(r_o.dtype)
        v_o[0, out] = v.astype(v_o.dtype)
        kk_o[0, out] = kk.astype(kk_o.dtype)
        bonus = jnp.zeros_like(r)
        for d in range(2):
            q = vec[d:d + 1] + _mm(jnp.tanh(lora[:, d * LANES:(d + 1) * LANES]), w2_ref[d])
            w_log = jnp.minimum(q, 0.0) - jnp.log1p(jnp.exp(-jnp.abs(q))) - 0.5
            a = jax.nn.sigmoid(vec[2 + d:3 + d]
                               + _mm(lora[:, (2 + d) * LANES:(3 + d) * LANES], w2_ref[2 + d]))
            kd = k * (1.0 + (a - 1.0) * k_a)
            kd_o[d, 0, out] = kd.astype(kd_o.dtype)
            b_o[d, 0, out] = (kk * a).astype(b_o.dtype)
            lw_o[d, 0, out] = -jnp.exp(w_log)
            bonus = bonus + _head_sum(r * kd * r_k, ones_bd)
        bonus_o[0, out] = (bonus * v).astype(bonus_o.dtype)
        zb_o[0, out] = _mm(hs[HALO:HALO + sub], win_ref[:, cols_a:]).astype(zb_o.dtype)


def _mix_prep(x, g0, sc, sh, w_in, conv, w0, w1, w2, a0, a1, a2, g1, g2, k_k, k_a, r_k, ones_bd,
              tm=512, sub=256):
    bsz, s, d = x.shape
    width = k_k.shape[0]
    lora = w1.shape[-1]
    cols_b = w_in.shape[1] - 4 * width
    pad_c = lambda m: jnp.pad(m, ((0, 0), (0, LANES - m.shape[1])))
    pad_r = lambda m: jnp.pad(m, ((0, LANES - m.shape[0]), (0, 0)))
    assert lora <= LANES and g1.shape[1] == LANES
    w1cat = jnp.concatenate([pad_c(w1[0]), pad_c(w1[1]), pad_c(a1[0]), pad_c(a1[1]), g1],
                            axis=1).astype(BF16)
    w2cat = jnp.stack([pad_r(w2[0]), pad_r(w2[1]), pad_r(a2[0]), pad_r(a2[1]), g2]).astype(BF16)
    vec = jnp.stack([w0[0], w0[1], a0[0], a0[1], k_k, k_a, r_k.reshape(-1),
                     jnp.zeros_like(k_k)])
    out1 = jax.ShapeDtypeStruct((bsz, s, width), BF16)
    out2 = jax.ShapeDtypeStruct((2, bsz, s, width), BF16)
    spec1 = pl.BlockSpec((1, tm, width), lambda b, i: (b, i, 0))
    spec2 = pl.BlockSpec((2, 1, tm, width), lambda b, i: (0, b, i, 0))
    mvec = pl.BlockSpec((1, 1, d), lambda b, i: (b, 0, 0))
    return pl.pallas_call(
        functools.partial(_mix_prep_kernel, tm=tm, sub=sub, width=width),
        grid=(bsz, s // tm),
        in_specs=(_halo_specs(tm, d, s)
                  + [pl.BlockSpec((1, d), lambda b, i: (0, 0)), mvec, mvec,
                     _resident(w_in.shape), _resident(conv.shape), _resident(w1cat.shape),
                     _resident(w2cat.shape), _resident(vec.shape), _resident(ones_bd.shape)]),
        out_specs=([pl.BlockSpec((1, tm, cols_b), lambda b, i: (b, i, 0))]
                   + [spec1] * 5 + [spec2] * 3),
        out_shape=([jax.ShapeDtypeStruct((bsz, s, cols_b), F32)] + [out1] * 5 + [out2] * 2
                   + [jax.ShapeDtypeStruct((2, bsz, s, width), F32)]),
        compiler_params=_cparams(("parallel", "parallel")),
        name="mix_in_rwkv_prep",
    )(x, x, x, g0.reshape(1, d), sc.reshape(bsz, 1, d), sh.reshape(bsz, 1, d),
      w_in.astype(BF16), conv, w1cat, w2cat, vec, ones_bd)


_BNN = (((2,), (1,)), ((0,), (0,)))
_BNT = (((2,), (2,)), ((0,), (0,)))
_BTN = (((1,), (1,)), ((0,), (0,)))


def _bmm(a, b, dims=_BNN):
    return lax.dot_general(a.astype(BF16), b.astype(BF16), dims, preferred_element_type=F32)


def _stack(x, head0):
    return jnp.concatenate([jnp.where(head0, x, 0.0), jnp.where(head0, 0.0, x)], axis=1)


def _wkv_group(r, k, v, kk, b, lw, h0, tri, m_strict, m_incl):
    grp, c, _ = r.shape
    n2 = 2 * c
    lw3 = _split(lw, 3)
    cum = sum(lax.dot_general(tri, p, _BNN, preferred_element_type=F32) for p in lw3)
    tot = jnp.sum(lw, axis=1, keepdims=True)
    ones_c = jnp.ones((grp, c, PAIR), BF16)
    tot_col = sum(lax.dot_general(p, ones_c, _BTN, preferred_element_type=F32) for p in lw3)
    g_inv = jnp.exp(-cum)
    g_rat = jnp.exp(tot - cum)
    lane = lax.broadcasted_iota(jnp.int32, (1, 1, PAIR), 2)
    head0 = lane < HEAD_DIM
    left = lane < c
    kt = kk * jnp.exp(cum - lw)
    rt = r * jnp.exp(cum)
    vs = _stack(v, head0)

    a_all = _bmm(jnp.concatenate([kt, rt], axis=1),
                 jnp.concatenate([_stack(k * g_inv, head0), _stack(b * g_inv, head0)], axis=1),
                 _BNT)
    akk = jnp.where(m_strict, a_all[:, :c, :n2], 0.0)
    akb = jnp.where(m_strict, a_all[:, :c, n2:], 0.0)
    ark = jnp.where(m_incl, a_all[:, c:, :n2], 0.0)
    arb = jnp.where(m_incl, a_all[:, c:, n2:], 0.0)

    eye = (lax.broadcasted_iota(jnp.int32, (1, c, n2), 1)
           == lax.broadcasted_iota(jnp.int32, (1, c, n2), 2) % c).astype(F32)
    n = -akb
    t = eye + n
    p = _bmm(n, _stack(n, left))
    for _ in range(int(np.ceil(np.log2(c))) - 2):
        pt = _bmm(jnp.concatenate([p, t], axis=1), _stack(p, left))
        p = pt[:, :c]
        t = t + pt[:, c:]
    t = t + _bmm(t, _stack(p, left))
    t_hi, t_lo = _split(t, 2)
    ia_t = lax.dot_general((eye + akb).astype(BF16),
                           jnp.concatenate([_stack(t_hi, left), _stack(t_lo, left)], axis=2),
                           _BNN, preferred_element_type=F32)
    t = t + _bmm(t, _stack(eye - ia_t[:, :, :n2] - ia_t[:, :, n2:], left))

    hk = _bmm(jnp.concatenate([kt, rt], axis=1), h0)
    av = _bmm(jnp.concatenate([akk, ark], axis=1), vs)
    u = _bmm(t, _stack(hk[:, :c] + av[:, :c], head0))
    y = hk[:, c:] + av[:, c:] - _bmm(arb, _stack(u, head0))
    same_head = ((lax.broadcasted_iota(jnp.int32, (1, PAIR, PAIR), 1) < HEAD_DIM)
                 == (lax.broadcasted_iota(jnp.int32, (1, PAIR, PAIR), 2) < HEAD_DIM))
    upd = _bmm(jnp.concatenate([k * g_rat, -(b * g_rat)], axis=1),
               jnp.concatenate([v, u], axis=1), _BTN)
    h_new = jnp.exp(tot_col) * h0 + jnp.where(same_head, upd, 0.0)
    return y, h_new


def _wkv_kernel(r0_ref, r1_ref, v0_ref, v1_ref, kk0_ref, kk1_ref, k0_ref, k1_ref, b0_ref, b1_ref,
                lw0_ref, lw1_ref, tri_ref, ms_ref, mi_ref, y0_ref, y1_ref, h_ref, *, n_pairs, nb):
    @pl.when(pl.program_id(1) == 0)
    def _():
        h_ref[...] = jnp.zeros_like(h_ref)

    cells = [(bi, slice(p * PAIR, (p + 1) * PAIR)) for bi in range(nb) for p in range(n_pairs)]

    def group(ref0, ref1):
        return jnp.stack([ref0[bi, :, sl] for bi, sl in cells]
                         + [ref1[bi, :, sl] for bi, sl in cells]).astype(F32)

    per_dir = lambda ref: jnp.stack([ref[0]] * len(cells) + [ref[1]] * len(cells))
    y, h_new = _wkv_group(group(r0_ref, r1_ref), group(k0_ref, k1_ref), group(v0_ref, v1_ref),
                          group(kk0_ref, kk1_ref), group(b0_ref, b1_ref), group(lw0_ref, lw1_ref),
                          h_ref[...], per_dir(tri_ref), per_dir(ms_ref) > 0.5, per_dir(mi_ref) > 0.5)
    h_ref[...] = h_new
    for i, (bi, sl) in enumerate(cells):
        y0_ref[bi, :, sl] = y[i]
        y1_ref[bi, :, sl] = y[len(cells) + i]


def _wkv_scan(r, v, kk, kd, bb, lw):
    bsz, s, width = r.shape
    c = CHUNK
    nc = s // c
    n_pairs = width // PAIR
    t_idx = np.arange(c)
    before = np.stack([t_idx[None, :] < t_idx[:, None], t_idx[None, :] > t_idx[:, None]])
    eye = np.eye(c, dtype=bool)[None]
    tri = jnp.asarray(before | eye, BF16)
    m_strict = jnp.asarray(np.tile(before, (1, 1, 2)), F32)
    m_incl = jnp.asarray(np.tile(before | eye, (1, 1, 2)), F32)

    nb = 2 if bsz % 2 == 0 else 1
    fwd = pl.BlockSpec((nb, c, width), lambda b, ci: (b, ci, 0))
    bwd = pl.BlockSpec((nb, c, width), lambda b, ci: (b, nc - 1 - ci, 0))
    fwd_d = pl.BlockSpec((None, nb, c, width), lambda b, ci: (0, b, ci, 0))
    bwd_d = pl.BlockSpec((None, nb, c, width), lambda b, ci: (1, b, nc - 1 - ci, 0))
    return pl.pallas_call(
        functools.partial(_wkv_kernel, n_pairs=n_pairs, nb=nb),
        grid=(bsz // nb, nc),
        in_specs=[fwd, bwd] * 3 + [fwd_d, bwd_d] * 3 + [
            _resident(tri.shape), _resident(m_strict.shape), _resident(m_incl.shape)],
        out_specs=[fwd, bwd],
        out_shape=[jax.ShapeDtypeStruct((bsz, s, width), F32)] * 2,
        scratch_shapes=[pltpu.VMEM((2 * nb * n_pairs, PAIR, PAIR), F32)],
        compiler_params=_cparams(("parallel", "arbitrary")),
        name="wkv_scan",
    )(r, r, v, v, kk, kk, kd, kd, bb, bb, lw, lw, tri, m_strict, m_incl)


def _mix_out_kernel(y0_ref, y1_ref, gate_ref, bonus_ref, lw_ref, lb_ref, ones_ref, yb_ref,
                    wa_ref, wb_ref, g_ref, gt_ref, x_ref, o_ref):
    ones_bd = ones_ref[...]
    y = y0_ref[0] + y1_ref[0]
    inv_n = 1.0 / HEAD_DIM
    yc = y - _head_sum(y, ones_bd) * inv_n
    var = _head_sum(yc * yc, ones_bd) * inv_n
    yn = yc * lax.rsqrt(var + LNX_EPS) * lw_ref[...] + lb_ref[...]
    ya = (yn + bonus_ref[0].astype(F32)) * gate_ref[0].astype(F32)
    acc = _mm(ya, wa_ref[...]) + _mm(yb_ref[0], wb_ref[...])
    o_ref[0] = x_ref[0] + gt_ref[0] * _rmsnorm(acc, g_ref[...])


def _mix_out(y0, y1, gate, bonus, lnx_w, lnx_b, ones_bd, yb, w_out, g, gt, x, tm=512):
    bsz, s, d = x.shape
    width = y0.shape[-1]
    row = lambda w: pl.BlockSpec((1, tm, w), lambda b, i: (b, i, 0))
    vecs = lambda w: pl.BlockSpec((1, w), lambda b, i: (0, 0))
    wa, wb = w_out[:width].astype(BF16), w_out[width:].astype(BF16)
    return pl.pallas_call(
        _mix_out_kernel,
        grid=(bsz, s // tm),
        in_specs=[row(width)] * 4 + [vecs(width), vecs(width), _resident(ones_bd.shape),
                                     row(yb.shape[-1]), _resident(wa.shape), _resident(wb.shape),
                                     vecs(d), pl.BlockSpec((1, 1, d), lambda b, i: (b, 0, 0)), row(d)],
        out_specs=row(d),
        out_shape=jax.ShapeDtypeStruct((bsz, s, d), F32),
        compiler_params=_cparams(("parallel", "parallel")),
        name="rwkv_post_mix_out",
    )(y0, y1, gate, bonus, lnx_w.reshape(1, width), lnx_b.reshape(1, width), ones_bd, yb,
      wa, wb, g.reshape(1, d), gt.reshape(bsz, 1, d), x)


def _rope_rows(x, cos, sin_signed):
    half = HEAD_DIM // 2
    lane = lax.broadcasted_iota(jnp.int32, (1, PAIR), 1)
    first_half = (lane % HEAD_DIM) < half
    partner = jnp.where(first_half, pltpu.roll(x, PAIR - half, axis=1), pltpu.roll(x, half, axis=1))
    return x * cos + partner * sin_signed


def _dilated_kernel(q_ref, k_ref, v_ref, cos_ref, sin_ref, o_ref,
                    qr, kp, vp, m_s, den_s, num_s, *, seq, pad, rope_rows):
    scale = HEAD_DIM ** -0.5
    zeros_pad = jnp.zeros((pad, PAIR), F32)
    kp[0:pad] = zeros_pad
    vp[0:pad] = zeros_pad
    kp[pad + seq:pad + seq + pad] = zeros_pad
    vp[pad + seq:pad + seq + pad] = zeros_pad

    def rope_body(j, carry):
        rows = pl.ds(pl.multiple_of(j * rope_rows, rope_rows), rope_rows)
        cos = cos_ref[rows, :]
        sin = sin_ref[rows, :]
        qr[rows, :] = _rope_rows(q_ref[0, rows, :], cos, sin) * scale
        dst = pl.ds(pl.multiple_of(pad + j * rope_rows, rope_rows), rope_rows)
        kp[dst, :] = _rope_rows(k_ref[0, rows, :], cos, sin)
        vp[dst, :] = v_ref[0, rows, :]
        return carry

    lax.fori_loop(0, seq // rope_rows, rope_body, 0)

    lane = lax.broadcasted_iota(jnp.int32, (1, PAIR), 1)
    head0 = lane < HEAD_DIM
    qi = lax.broadcasted_iota(jnp.int32, (QBLK, 1), 0)
    col = lax.broadcasted_iota(jnp.int32, (1, 2 * KBLK), 1)
    kj = col % KBLK
    left = col < KBLK
    half_w = QBLK // 2
    band = jnp.abs(kj - half_w - qi) <= half_w
    ones_st = ((lax.broadcasted_iota(jnp.int32, (2 * KBLK, PAIR), 0) < KBLK)
               == (lax.broadcasted_iota(jnp.int32, (2 * KBLK, PAIR), 1) < HEAD_DIM)).astype(BF16)
    stack_rows = lambda x: jnp.concatenate([jnp.where(head0, x, 0.0), jnp.where(head0, 0.0, x)],
                                           axis=0).astype(BF16)

    for branch, (window, dil) in enumerate(DILATED_PAIRS):
        assert window // (2 * dil) == half_w
        m_len = seq // dil
        blocks_per_res = m_len // QBLK

        def group_body(it, carry, dil=dil, m_len=m_len, blocks_per_res=blocks_per_res,
                       first=(branch == 0)):
            q_rows, qs, ks, vs, in_range = [], [], [], [], []
            for g in range(ATTN_GROUP):
                blk = it * ATTN_GROUP + g
                res = blk // blocks_per_res
                m0 = (blk % blocks_per_res) * QBLK
                q_start = res + dil * m0
                k_start = pad + res + dil * (m0 - half_w)
                if dil == 1:
                    rows = pl.ds(pl.multiple_of(q_start, QBLK), QBLK)
                    k_rows = pl.ds(pl.multiple_of(k_start, half_w), KBLK)
                else:
                    rows = pl.ds(q_start, QBLK, stride=dil)
                    k_rows = pl.ds(k_start, KBLK, stride=dil)
                q_rows.append(rows)
                qs.append(qr[rows, :].astype(BF16))
                ks.append(stack_rows(kp[k_rows, :]))
                vs.append(jnp.concatenate([stack_rows(vp[k_rows, :]), ones_st], axis=1))
                kpos = m0 - half_w + kj
                in_range.append((kpos >= 0) & (kpos < m_len))
            sc = _bmm(jnp.stack(qs), jnp.stack(ks), _BNT)
            sc = jnp.stack([jnp.where(in_range[g], jnp.where(band, sc[g], NEG_BIG), NEG_BIG)
                            for g in range(ATTN_GROUP)])
            tile_max = lambda x: jnp.maximum(x[:, :, :KBLK // 2], x[:, :, KBLK // 2:])
            mx0 = jnp.max(tile_max(sc[:, :, :KBLK]), axis=-1, keepdims=True)
            mx1 = jnp.max(tile_max(sc[:, :, KBLK:]), axis=-1, keepdims=True)
            p = jnp.exp(sc - jnp.where(left, mx0, mx1)).astype(BF16)
            nd = lax.dot_general(p, jnp.stack(vs), _BNN, preferred_element_type=F32)
            n_b = nd[:, :, :PAIR]
            d_b = nd[:, :, PAIR:]
            m_b = jnp.where(head0, mx0, mx1)
            for g, rows in enumerate(q_rows):
                if first:
                    m_s[rows, :] = m_b[g]
                    den_s[rows, :] = d_b[g]
                    num_s[rows, :] = n_b[g]
                else:
                    m_old = m_s[rows, :]
                    m_new = jnp.maximum(m_old, m_b[g])
                    alpha = jnp.exp(m_old - m_new)
                    beta = jnp.exp(m_b[g] - m_new)
                    m_s[rows, :] = m_new
                    den_s[rows, :] = alpha * den_s[rows, :] + beta * d_b[g]
                    num_s[rows, :] = alpha * num_s[rows, :] + beta * n_b[g]
            return carry

        lax.fori_loop(0, seq // (QBLK * ATTN_GROUP), group_body, 0)

    def out_body(j, carry):
        rows = pl.ds(pl.multiple_of(j * rope_rows, rope_rows), rope_rows)
        o_ref[0, rows, :] = (num_s[rows, :] / den_s[rows, :]).astype(o_ref.dtype)
        return carry

    lax.fori_loop(0, seq // rope_rows, out_body, 0)


def _dilated_attention(z, col0, width):
    bsz, s, _ = z.shape
    n_pairs = width // PAIR
    max_dil = max(d for _, d in DILATED_PAIRS)
    assert s % (QBLK * max_dil) == 0 and s % (QBLK * ATTN_GROUP) == 0 and col0 % PAIR == 0
    pad = (QBLK // 2) * max_dil
    half = HEAD_DIM // 2
    inv = ROPE_THETA ** (-jnp.arange(half, dtype=F32) / half)
    ang = jnp.arange(s, dtype=F32)[:, None] * inv[None, :]
    cos = jnp.tile(jnp.cos(ang), (1, PAIR // half))
    sin = jnp.tile(jnp.concatenate([-jnp.sin(ang), jnp.sin(ang)], axis=1), (1, 2))
    cb = col0 // PAIR
    col = lambda off: pl.BlockSpec((1, s, PAIR), lambda b, p: (b, 0, cb + off * n_pairs + p))
    rope_rows = 512
    return pl.pallas_call(
        functools.partial(_dilated_kernel, seq=s, pad=pad, rope_rows=rope_rows),
        grid=(bsz, n_pairs),
        in_specs=[col(0), col(1), col(2), _resident((s, PAIR)), _resident((s, PAIR))],
        out_specs=pl.BlockSpec((1, s, PAIR), lambda b, p: (b, 0, p)),
        out_shape=jax.ShapeDtypeStruct((bsz, s, width), F32),
        scratch_shapes=[pltpu.VMEM((s, PAIR), F32),
                        pltpu.VMEM((s + 2 * pad, PAIR), F32),
                        pltpu.VMEM((s + 2 * pad, PAIR), F32),
                        pltpu.VMEM((s, PAIR), F32),
                        pltpu.VMEM((s, PAIR), F32),
                        pltpu.VMEM((s, PAIR), F32)],
        compiler_params=_cparams(("parallel", "parallel")),
        name="dilated_attention",
    )(z, z, z, cos, sin)


def _seq_dft_kernel(xc_ref, xs_ref, cm_ref, sm_ref, jm_ref, alt_ref, o_ref, xe_s, yo_s,
                    *, seq, t):
    h = seq // 2
    nb = h // t
    scale = seq ** -0.5
    jm = jm_ref[...]
    row0 = lax.broadcasted_iota(jnp.int32, (t, 1), 0) == 0
    for i in range(nb):
        rows = slice(i * t, (i + 1) * t)
        mirror = slice(seq - (i + 1) * t, seq - i * t)
        pc = jnp.dot(jm, xc_ref[0, mirror, :], preferred_element_type=F32)
        ps = jnp.dot(jm, xs_ref[0, mirror, :], preferred_element_type=F32)
        if i > 0:
            first = slice(seq - i * t, seq - i * t + 1)
            pc = jnp.where(row0, xc_ref[0, first, :].astype(F32), pc)
            ps = jnp.where(row0, xs_ref[0, first, :].astype(F32), ps)
        xe_s[rows, :] = (xc_ref[0, rows, :].astype(F32) + pc).astype(BF16)
        yo_s[rows, :] = (xs_ref[0, rows, :].astype(F32) - ps).astype(BF16)
    xe = xe_s[...]
    p = jnp.dot(cm_ref[...], xe, preferred_element_type=F32)
    q = jnp.dot(sm_ref[...], yo_s[...], preferred_element_type=F32)
    alt = alt_ref[...]
    x_h = xc_ref[0, h:h + 1, :].astype(F32) * scale
    k_odd = lax.broadcasted_iota(jnp.int32, (h, 1), 0) % 2 == 1
    alt_x = jnp.where(k_odd, -x_h, x_h)
    o_ref[0, 0:h, :] = (p - q + alt_x).astype(o_ref.dtype)
    pq = (p + q + alt_x).astype(BF16)
    f_h = scale * jnp.dot(alt, xe, preferred_element_type=F32)[0:1] + x_h
    for i in range(nb):
        rev = jnp.dot(jm, pq[h - (i + 1) * t:h - i * t], preferred_element_type=F32)
        first = f_h if i == 0 else pq[h - i * t:h - i * t + 1].astype(F32)
        o_ref[0, h + i * t:h + (i + 1) * t, :] = jnp.where(row0, first, rev).astype(o_ref.dtype)


def _seq_dft(cm, sm, x2, tn=256, t=128):
    bsz, s, two_d = x2.shape
    d = two_d // 2
    h = s // 2
    assert h % 2 == 0 and h % t == 0 and d % tn == 0
    r = np.arange(t)
    jm = jnp.asarray((r[None, :] == t - r[:, None]) & (r[:, None] >= 1), BF16)
    alt = jnp.asarray(np.where(np.arange(8)[:, None] == 0, 1.0 - 2.0 * (np.arange(h) % 2), 0.0), BF16)
    nj = d // tn
    return pl.pallas_call(
        functools.partial(_seq_dft_kernel, seq=s, t=t),
        grid=(bsz, nj),
        in_specs=[pl.BlockSpec((1, s, tn), lambda b, j: (b, 0, j)),
                  pl.BlockSpec((1, s, tn), lambda b, j: (b, 0, nj + j)),
                  _resident((h, h)), _resident((h, h)), _resident((t, t)), _resident((8, h))],
        out_specs=pl.BlockSpec((1, s, tn), lambda b, j: (b, 0, j)),
        out_shape=jax.ShapeDtypeStruct((bsz, s, d), BF16),
        scratch_shapes=[pltpu.VMEM((h, tn), BF16), pltpu.VMEM((h, tn), BF16)],
        compiler_params=_cparams(("parallel", "parallel")),
        name="seq_dft",
    )(x2, x2, cm, sm, jm, alt)


def _dft_tables(s, d):
    gsz = d // N_FOURIER_GROUPS
    cidx = jnp.arange(gsz, dtype=jnp.int32)
    ang_c = (2.0 * np.pi / gsz) * ((cidx[:, None] * cidx[None, :]) % gsz).astype(F32)
    w_ch = jnp.concatenate([jnp.cos(ang_c), jnp.sin(ang_c)], axis=1) * (gsz ** -0.5)
    q = 64
    h = s // 2
    assert h % q == 0
    sidx = jnp.arange(h, dtype=jnp.int32)[None, :]
    ang = lambda kvals: (2.0 * np.pi / s) * ((kvals[:, None] * sidx) % s).astype(F32)
    ang_hi = ang(q * jnp.arange(h // q, dtype=jnp.int32))[:, None, :]
    ang_lo = ang(jnp.arange(q, dtype=jnp.int32))[None, :, :]
    scale = s ** -0.5
    ch, sh, cl, sl = jnp.cos(ang_hi) * scale, jnp.sin(ang_hi) * scale, jnp.cos(ang_lo), jnp.sin(ang_lo)
    cm = (ch * cl - sh * sl).reshape(h, h).astype(BF16)
    sm = (sh * cl + ch * sl).reshape(h, h).astype(BF16)
    return w_ch.astype(BF16), cm, sm


def kernel(x, c, ada_w, ada_b, norm_g, mix_in, mix_conv, rwkv_w0, rwkv_w1, rwkv_w2, rwkv_a0, rwkv_a1, rwkv_a2, rwkv_g1, rwkv_g2, rwkv_k_k, rwkv_k_a, rwkv_r_k, rwkv_lnx_w, rwkv_lnx_b, mix_out, fnet_w, ffn_up, ffn_conv, ffn_down):
    bsz, s, d = x.shape
    depth = ada_w.shape[0]
    width_a = rwkv_k_k.shape[-1]
    width_b = mix_out.shape[1] - width_a
    in_cols_a = 4 * width_a
    ones_bd = jnp.kron(jnp.eye(width_a // HEAD_DIM, dtype=F32),
                       jnp.ones((HEAD_DIM, HEAD_DIM), F32)).astype(BF16)

    mod = _adaln_mod(c, ada_w, ada_b)
    for l in range(depth):
        sh1, sc1, gt1, sh2, sc2, gt2 = [mod[l, :, j * d:(j + 1) * d] for j in range(6)]
        if l % 2 == 0:
            e = l // 2
            zb, r, v, kk, g, bonus, kd, bb, lw = _mix_prep(
                x, norm_g[l, 0], sc1, sh1, mix_in[e], mix_conv[e], rwkv_w0[e], rwkv_w1[e],
                rwkv_w2[e], rwkv_a0[e], rwkv_a1[e], rwkv_a2[e], rwkv_g1[e], rwkv_g2[e],
                rwkv_k_k[e], rwkv_k_a[e], rwkv_r_k[e], ones_bd)
            y0, y1 = _wkv_scan(r, v, kk, kd, bb, lw)
            yb = _dilated_attention(zb, 0, width_b)
            x = _mix_out(y0, y1, g, bonus, rwkv_lnx_w[e], rwkv_lnx_b[e], ones_bd, yb,
                         mix_out[e], norm_g[l, 1], gt1, x)
        else:
            w_ch, cm, sm = _dft_tables(s, d)
            x2 = _chan_dft(x, norm_g[l, 0], sc1, sh1, w_ch)
            f = _seq_dft(cm, sm, x2)
            x = _out_proj([f], [fnet_w[l // 2]], norm_g[l, 1], gt1, x)
        x = _ffn(x, norm_g[l, 2], sc2, sh2, ffn_up[l], ffn_conv[l], ffn_down[l],
                 norm_g[l, 3], gt2)
    return x
```

```python
import functools

import jax
import jax.numpy as jnp
import numpy as np
from jax import lax
from jax.experimental import pallas as pl
from jax.experimental.pallas import tpu as pltpu

F32 = jnp.float32
BF16 = jnp.bfloat16

HEAD_DIM = 64
DILATED_PAIRS = ((128, 1), (512, 4), (2048, 16))
ROPE_THETA = 10000.0
N_FOURIER_GROUPS = 8
RMS_EPS = 1e-6
LNX_EPS = 64e-5
L2_EPS = 1e-12
NEG_BIG = -1e30

LANES = 128
SUBLANES = 8
VMEM_LIMIT_BYTES = 56 * 1024 * 1024

PAIR = 2 * HEAD_DIM
assert PAIR == LANES
HALO = SUBLANES
CHUNK = 64
QBLK = 128
KBLK = 2 * QBLK
ATTN_GROUP = 4


def _cparams(sem):
    return pltpu.CompilerParams(dimension_semantics=sem, vmem_limit_bytes=VMEM_LIMIT_BYTES)


def _resident(shape):
    zeros = (0,) * len(shape)
    return pl.BlockSpec(shape, lambda *_: zeros, pipeline_mode=pl.Buffered(1))


def _mm(a, b):
    return jnp.dot(a.astype(BF16), b.astype(BF16), preferred_element_type=F32)


def _mm_nt(a, b):
    return lax.dot_general(a.astype(BF16), b.astype(BF16), (((1,), (1,)), ((), ())),
                           preferred_element_type=F32)


def _split(x, n):
    parts = []
    rem = x
    for _ in range(n):
        p = rem.astype(BF16)
        parts.append(p)
        rem = rem - p.astype(F32)
    return parts


def _rmsnorm(x, g):
    return x * lax.rsqrt(jnp.mean(x * x, -1, keepdims=True) + RMS_EPS) * g


def _conv3_rows(z, w, n_rows):
    total = z.shape[0]
    y = pltpu.roll(z, 1, axis=0) * w[0:1] + z * w[1:2] + pltpu.roll(z, total - 1, axis=0) * w[2:3]
    return y[HALO:HALO + n_rows]


def _halo_specs(tm, width, seq, col_block=0):
    per = tm // HALO
    last = seq // HALO - 1

    def prev_map(b, i):
        return (b, jnp.maximum(i * per - 1, 0), col_block)

    def next_map(b, i):
        return (b, jnp.minimum((i + 1) * per, last), col_block)

    return [pl.BlockSpec((1, HALO, width), prev_map),
            pl.BlockSpec((1, tm, width), lambda b, i: (b, i, col_block)),
            pl.BlockSpec((1, HALO, width), next_map)]


def _mod_kernel(c_ref, w_ref, b_ref, o_ref):
    c = c_ref[...]
    cs = c * jax.nn.sigmoid(c)
    o_ref[0] = jnp.dot(cs, w_ref[0], precision=lax.Precision.HIGHEST,
                       preferred_element_type=F32) + b_ref[0]


def _adaln_mod(c, ada_w, ada_b):
    depth, d, n = ada_w.shape
    b = c.shape[0]
    tn = n // 4
    return pl.pallas_call(
        _mod_kernel,
        grid=(depth, n // tn),
        in_specs=[pl.BlockSpec((b, d), lambda l, j: (0, 0)),
                  pl.BlockSpec((1, d, tn), lambda l, j: (l, 0, j)),
                  pl.BlockSpec((1, 1, tn), lambda l, j: (l, 0, j))],
        out_specs=pl.BlockSpec((1, b, tn), lambda l, j: (l, 0, j)),
        out_shape=jax.ShapeDtypeStruct((depth, b, n), F32),
        compiler_params=_cparams(("parallel", "parallel")),
        name="adaln_mod",
    )(c, ada_w, ada_b.reshape(depth, 1, n))


def _chan_dft_kernel(x_ref, g_ref, sc_ref, sh_ref, w_ref, o_ref):
    h = (_rmsnorm(x_ref[0], g_ref[...]) * (1.0 + sc_ref[0]) + sh_ref[0]).astype(BF16)
    d = h.shape[1]
    gsz = w_ref.shape[0]
    w = w_ref[...]
    for grp in range(d // gsz):
        cols = slice(grp * gsz, (grp + 1) * gsz)
        xy = jnp.dot(h[:, cols], w, preferred_element_type=F32)
        o_ref[0, :, cols] = xy[:, :gsz].astype(o_ref.dtype)
        o_ref[0, :, d + grp * gsz:d + (grp + 1) * gsz] = xy[:, gsz:].astype(o_ref.dtype)


def _chan_dft(x, g, sc, sh, w, tm=512):
    bsz, s, d = x.shape
    assert w.shape[0] % LANES == 0 and d % w.shape[0] == 0
    vec = pl.BlockSpec((1, 1, d), lambda b, i: (b, 0, 0))
    return pl.pallas_call(
        _chan_dft_kernel,
        grid=(bsz, s // tm),
        in_specs=[pl.BlockSpec((1, tm, d), lambda b, i: (b, i, 0)),
                  pl.BlockSpec((1, d), lambda b, i: (0, 0)), vec, vec,
                  _resident(w.shape)],
        out_specs=pl.BlockSpec((1, tm, 2 * d), lambda b, i: (b, i, 0)),
        out_shape=jax.ShapeDtypeStruct((bsz, s, 2 * d), BF16),
        compiler_params=_cparams(("parallel", "parallel")),
        name="chan_dft",
    )(x, g.reshape(1, d), sc.reshape(bsz, 1, d), sh.reshape(bsz, 1, d), w)


def _out_proj_kernel(*refs, n_in):
    y_refs = refs[:n_in]
    w_refs = refs[n_in:2 * n_in]
    g_ref, gt_ref, x_ref, o_ref = refs[2 * n_in:]
    acc = _mm(y_refs[0][0], w_refs[0][...])
    for y_ref, w_ref in zip(y_refs[1:], w_refs[1:]):
        acc = acc + _mm(y_ref[0], w_ref[...])
    o_ref[0] = x_ref[0] + gt_ref[0] * _rmsnorm(acc, g_ref[...])


def _out_proj(ys, ws, g, gt, x, tm=512):
    bsz, s, d = x.shape
    n_in = len(ys)
    row = lambda width: pl.BlockSpec((1, tm, width), lambda b, i: (b, i, 0))
    return pl.pallas_call(
        functools.partial(_out_proj_kernel, n_in=n_in),
        grid=(bsz, s // tm),
        in_specs=([row(y.shape[-1]) for y in ys] + [_resident(w.shape) for w in ws]
                  + [pl.BlockSpec((1, d), lambda b, i: (0, 0)),
                     pl.BlockSpec((1, 1, d), lambda b, i: (b, 0, 0)), row(d)]),
        out_specs=row(d),
        out_shape=jax.ShapeDtypeStruct((bsz, s, d), F32),
        compiler_params=_cparams(("parallel", "parallel")),
        name="out_proj",
    )(*ys, *[w.astype(BF16) for w in ws], g.reshape(1, d), gt.reshape(bsz, 1, d), x)


def _ffn_kernel(xp_ref, x_ref, xn_ref, g0_ref, sc_ref, sh_ref, up_ref, cw_ref, dn_ref,
                g1_ref, gt_ref, o_ref, *, tm, d_ff, n_chunks):
    i = pl.program_id(1)
    last = pl.num_programs(1) - 1
    x = x_ref[0]
    xh = jnp.concatenate([xp_ref[0], x, xn_ref[0]], axis=0)
    h = _rmsnorm(xh, g0_ref[...]) * (1.0 + sc_ref[0]) + sh_ref[0]
    rows = lax.broadcasted_iota(jnp.int32, (tm + 2 * HALO, 1), 0)
    lo = jnp.where(i > 0, 0, HALO)
    hi = jnp.where(i < last, tm + 2 * HALO, tm + HALO)
    h = jnp.where((rows >= lo) & (rows < hi), h, 0.0).astype(BF16)
    fc = d_ff // n_chunks
    acc = jnp.zeros((tm, x.shape[1]), F32)
    for f in range(n_chunks):
        gsl = slice(f * fc, (f + 1) * fc)
        vsl = slice(d_ff + f * fc, d_ff + (f + 1) * fc)
        zg = _conv3_rows(_mm(h, up_ref[:, gsl]), cw_ref[:, gsl], tm)
        zv = _conv3_rows(_mm(h, up_ref[:, vsl]), cw_ref[:, vsl], tm)
        act = jax.nn.gelu(zg, approximate=True) * zv
        acc = acc + _mm(act, dn_ref[gsl, :])
    o_ref[0] = x + gt_ref[0] * _rmsnorm(acc, g1_ref[...])


def _ffn(x, g0, sc, sh, up, conv, down, g1, gt, tm=512, n_chunks=2):
    bsz, s, d = x.shape
    d_ff = down.shape[0]
    vec = pl.BlockSpec((1, 1, d), lambda b, i: (b, 0, 0))
    gvec = pl.BlockSpec((1, d), lambda b, i: (0, 0))
    return pl.pallas_call(
        functools.partial(_ffn_kernel, tm=tm, d_ff=d_ff, n_chunks=n_chunks),
        grid=(bsz, s // tm),
        in_specs=(_halo_specs(tm, d, s)
                  + [gvec, vec, vec, _resident(up.shape), _resident(conv.shape),
                     _resident(down.shape), gvec, vec]),
        out_specs=pl.BlockSpec((1, tm, d), lambda b, i: (b, i, 0)),
        out_shape=jax.ShapeDtypeStruct((bsz, s, d), F32),
        compiler_params=_cparams(("parallel", "parallel")),
        name="conv_glu_ffn",
    )(x, x, x, g0.reshape(1, d), sc.reshape(bsz, 1, d), sh.reshape(bsz, 1, d),
      up.astype(BF16), conv, down.astype(BF16), g1.reshape(1, d), gt.reshape(bsz, 1, d))


def _head_sum(x, ones_bd):
    hi, lo = _split(x, 2)
    return (jnp.dot(hi, ones_bd, preferred_element_type=F32)
            + jnp.dot(lo, ones_bd, preferred_element_type=F32))


def _mix_prep_kernel(xp_ref, x_ref, xn_ref, g0_ref, sc_ref, sh_ref, win_ref, cw_ref, w1_ref,
                     w2_ref, vec_ref, ones_ref, zb_o, r_o, v_o, kk_o, g_o, bonus_o, kd_o, b_o,
                     lw_o, *, tm, sub, width):
    i = pl.program_id(1)
    last = pl.num_programs(1) - 1
    xh = jnp.concatenate([xp_ref[0], x_ref[0], xn_ref[0]], axis=0)
    h = _rmsnorm(xh, g0_ref[...]) * (1.0 + sc_ref[0]) + sh_ref[0]
    rows = lax.broadcasted_iota(jnp.int32, (tm + 2 * HALO, 1), 0)
    lo = jnp.where(i > 0, 0, HALO)
    hi = jnp.where(i < last, tm + 2 * HALO, tm + HALO)
    h = jnp.where((rows >= lo) & (rows < hi), h, 0.0).astype(BF16)
    cols_a = 4 * width
    ones_bd = ones_ref[...]
    vec = vec_ref[...]
    k_k, k_a, r_k = vec[4:5], vec[5:6], vec[6:7]
    for j in range(tm // sub):
        out = slice(j * sub, (j + 1) * sub)
        hs = h[j * sub:(j + 1) * sub + 2 * HALO]
        za = _conv3_rows(_mm(hs, win_ref[:, :cols_a]), cw_ref[...], sub)
        r = za[:, 0:width]
        k = za[:, width:2 * width]
        v = za[:, 2 * width:3 * width]
        u = za[:, 3 * width:4 * width]

        lora = _mm(u, w1_ref[...])
        g_o[0, out] = _mm(jax.nn.sigmoid(lora[:, 4 * LANES:5 * LANES]), w2_ref[4]).astype(g_o.dtype)
        kk = k * k_k
        kk = kk * lax.rsqrt(_head_sum(kk * kk, ones_bd) + L2_EPS)
        r_o[0, out] = r.astype(r_o.dtype)
        v_o[0, out] = v.astype(v_o.dtype)
        kk_o[0, out] = kk.astype(kk_o.dtype)
        bonus = jnp.zeros_like(r)
        for d in range(2):
            q = vec[d:d + 1] + _mm(jnp.tanh(lora[:, d * LANES:(d + 1) * LANES]), w2_ref[d])
            w_log = jnp.minimum(q, 0.0) - jnp.log1p(jnp.exp(-jnp.abs(q))) - 0.5
            a = jax.nn.sigmoid(vec[2 + d:3 + d]
                               + _mm(lora[:, (2 + d) * LANES:(3 + d) * LANES], w2_ref[2 + d]))
            kd = k * (1.0 + (a - 1.0) * k_a)
            kd_o[d, 0, out] = kd.astype(kd_o.dtype)
            b_o[d, 0, out] = (kk * a).astype(b_o.dtype)
            lw_o[d, 0, out] = -jnp.exp(w_log)
            bonus = bonus + _head_sum(r * kd * r_k, ones_bd)
        bonus_o[0, out] = (bonus * v).astype(bonus_o.dtype)
        zb_o[0, out] = _mm(hs[HALO:HALO + sub], win_ref[:, cols_a:]).astype(zb_o.dtype)


def _mix_prep(x, g0, sc, sh, w_in, conv, w0, w1, w2, a0, a1, a2, g1, g2, k_k, k_a, r_k, ones_bd,
              tm=512, sub=256):
    bsz, s, d = x.shape
    width = k_k.shape[0]
    lora = w1.shape[-1]
    cols_b = w_in.shape[1] - 4 * width
    pad_c = lambda m: jnp.pad(m, ((0, 0), (0, LANES - m.shape[1])))
    pad_r = lambda m: jnp.pad(m, ((0, LANES - m.shape[0]), (0, 0)))
    assert lora <= LANES and g1.shape[1] == LANES
    w1cat = jnp.concatenate([pad_c(w1[0]), pad_c(w1[1]), pad_c(a1[0]), pad_c(a1[1]), g1],
                            axis=1).astype(BF16)
    w2cat = jnp.stack([pad_r(w2[0]), pad_r(w2[1]), pad_r(a2[0]), pad_r(a2[1]), g2]).astype(BF16)
    vec = jnp.stack([w0[0], w0[1], a0[0], a0[1], k_k, k_a, r_k.reshape(-1),
                     jnp.zeros_like(k_k)])
    out1 = jax.ShapeDtypeStruct((bsz, s, width), BF16)
    out2 = jax.ShapeDtypeStruct((2, bsz, s, width), BF16)
    spec1 = pl.BlockSpec((1, tm, width), lambda b, i: (b, i, 0))
    spec2 = pl.BlockSpec((2, 1, tm, width), lambda b, i: (0, b, i, 0))
    mvec = pl.BlockSpec((1, 1, d), lambda b, i: (b, 0, 0))
    return pl.pallas_call(
        functools.partial(_mix_prep_kernel, tm=tm, sub=sub, width=width),
        grid=(bsz, s // tm),
        in_specs=(_halo_specs(tm, d, s)
                  + [pl.BlockSpec((1, d), lambda b, i: (0, 0)), mvec, mvec,
                     _resident(w_in.shape), _resident(conv.shape), _resident(w1cat.shape),
                     _resident(w2cat.shape), _resident(vec.shape), _resident(ones_bd.shape)]),
        out_specs=([pl.BlockSpec((1, tm, cols_b), lambda b, i: (b, i, 0))]
                   + [spec1] * 5 + [spec2] * 3),
        out_shape=([jax.ShapeDtypeStruct((bsz, s, cols_b), BF16)] + [out1] * 5 + [out2] * 2
                   + [jax.ShapeDtypeStruct((2, bsz, s, width), F32)]),
        compiler_params=_cparams(("parallel", "parallel")),
        name="mix_in_rwkv_prep",
    )(x, x, x, g0.reshape(1, d), sc.reshape(bsz, 1, d), sh.reshape(bsz, 1, d),
      w_in.astype(BF16), conv, w1cat, w2cat, vec, ones_bd)


_BNN = (((2,), (1,)), ((0,), (0,)))
_BNT = (((2,), (2,)), ((0,), (0,)))
_BTN = (((1,), (1,)), ((0,), (0,)))


def _bmm(a, b, dims=_BNN):
    return lax.dot_general(a.astype(BF16), b.astype(BF16), dims, preferred_element_type=F32)


def _stack(x, head0):
    return jnp.concatenate([jnp.where(head0, x, 0.0), jnp.where(head0, 0.0, x)], axis=1)


def _wkv_group(r, k, v, kk, b, lw, ht0, tri, m_strict, m_incl):
    _, c, _ = r.shape
    n2 = 2 * c
    lw3 = _split(lw, 3)
    cum = sum(lax.dot_general(tri, p, _BNN, preferred_element_type=F32) for p in lw3)
    tot = jnp.sum(lw, axis=1, keepdims=True)
    g_inv = jnp.exp(-cum)
    g_rat = jnp.exp(tot - cum)
    lane = lax.broadcasted_iota(jnp.int32, (1, 1, PAIR), 2)
    head0 = lane < HEAD_DIM
    left = lane < c
    kt = kk * jnp.exp(cum - lw)
    rt = r * jnp.exp(cum)
    vs = _stack(v, head0)

    a_all = _bmm(jnp.concatenate([kt, rt], axis=1),
                 jnp.concatenate([_stack(k * g_inv, head0), _stack(b * g_inv, head0)], axis=1),
                 _BNT)
    akk = jnp.where(m_strict, a_all[:, :c, :n2], 0.0)
    akb = jnp.where(m_strict, a_all[:, :c, n2:], 0.0)
    ark = jnp.where(m_incl, a_all[:, c:, :n2], 0.0)
    arb = jnp.where(m_incl, a_all[:, c:, n2:], 0.0)

    eye = (lax.broadcasted_iota(jnp.int32, (1, c, n2), 1)
           == lax.broadcasted_iota(jnp.int32, (1, c, n2), 2) % c).astype(F32)
    n = -akb
    t = eye + n
    p = _bmm(n, _stack(n, left))
    for _ in range(int(np.ceil(np.log2(c))) - 2):
        pt = _bmm(jnp.concatenate([p, t], axis=1), _stack(p, left))
        p = pt[:, :c]
        t = t + pt[:, c:]
    t = t + _bmm(t, _stack(p, left))
    t_hi, t_lo = _split(t, 2)
    ia_t = lax.dot_general((eye + akb).astype(BF16),
                           jnp.concatenate([_stack(t_hi, left), _stack(t_lo, left)], axis=2),
                           _BNN, preferred_element_type=F32)
    t = t + _bmm(t, _stack(eye - ia_t[:, :, :n2] - ia_t[:, :, n2:], left))

    hk = _bmm(jnp.concatenate([kt, rt], axis=1), ht0, _BNT)
    av = _bmm(jnp.concatenate([akk, ark], axis=1), vs)
    u = _bmm(t, _stack(hk[:, :c] + av[:, :c], head0))
    y = hk[:, c:] + av[:, c:] - _bmm(arb, _stack(u, head0))
    same_head = ((lax.broadcasted_iota(jnp.int32, (1, PAIR, PAIR), 1) < HEAD_DIM)
                 == (lax.broadcasted_iota(jnp.int32, (1, PAIR, PAIR), 2) < HEAD_DIM))
    upd_t = _bmm(jnp.concatenate([v, u], axis=1),
                 jnp.concatenate([k * g_rat, -(b * g_rat)], axis=1), _BTN)
    ht_new = jnp.exp(tot) * ht0 + jnp.where(same_head, upd_t, 0.0)
    return y, ht_new


def _wkv_kernel(r0_ref, r1_ref, v0_ref, v1_ref, kk0_ref, kk1_ref, k0_ref, k1_ref, b0_ref, b1_ref,
                lw0_ref, lw1_ref, tri_ref, ms_ref, mi_ref, y0_ref, y1_ref, h_ref, *, n_pairs, nb):
    @pl.when(pl.program_id(1) == 0)
    def _():
        h_ref[...] = jnp.zeros_like(h_ref)

    cells = [(bi, slice(p * PAIR, (p + 1) * PAIR)) for bi in range(nb) for p in range(n_pairs)]

    def group(ref0, ref1):
        return jnp.stack([ref0[bi, :, sl] for bi, sl in cells]
                         + [ref1[bi, :, sl] for bi, sl in cells]).astype(F32)

    per_dir = lambda ref: jnp.stack([ref[0]] * len(cells) + [ref[1]] * len(cells))
    y, h_new = _wkv_group(group(r0_ref, r1_ref), group(k0_ref, k1_ref), group(v0_ref, v1_ref),
                          group(kk0_ref, kk1_ref), group(b0_ref, b1_ref), group(lw0_ref, lw1_ref),
                          h_ref[...], per_dir(tri_ref), per_dir(ms_ref) > 0.5, per_dir(mi_ref) > 0.5)
    h_ref[...] = h_new
    for i, (bi, sl) in enumerate(cells):
        y0_ref[bi, :, sl] = y[i]
        y1_ref[bi, :, sl] = y[len(cells) + i]


def _wkv_scan(r, v, kk, kd, bb, lw):
    bsz, s, width = r.shape
    c = CHUNK
    nc = s // c
    n_pairs = width // PAIR
    t_idx = np.arange(c)
    before = np.stack([t_idx[None, :] < t_idx[:, None], t_idx[None, :] > t_idx[:, None]])
    eye = np.eye(c, dtype=bool)[None]
    tri = jnp.asarray(before | eye, BF16)
    m_strict = jnp.asarray(np.tile(before, (1, 1, 2)), F32)
    m_incl = jnp.asarray(np.tile(before | eye, (1, 1, 2)), F32)

    nb = 2 if bsz % 2 == 0 else 1
    fwd = pl.BlockSpec((nb, c, width), lambda b, ci: (b, ci, 0))
    bwd = pl.BlockSpec((nb, c, width), lambda b, ci: (b, nc - 1 - ci, 0))
    fwd_d = pl.BlockSpec((None, nb, c, width), lambda b, ci: (0, b, ci, 0))
    bwd_d = pl.BlockSpec((None, nb, c, width), lambda b, ci: (1, b, nc - 1 - ci, 0))
    return pl.pallas_call(
        functools.partial(_wkv_kernel, n_pairs=n_pairs, nb=nb),
        grid=(bsz // nb, nc),
        in_specs=[fwd, bwd] * 3 + [fwd_d, bwd_d] * 3 + [
            _resident(tri.shape), _resident(m_strict.shape), _resident(m_incl.shape)],
        out_specs=[fwd, bwd],
        out_shape=[jax.ShapeDtypeStruct((bsz, s, width), F32)] * 2,
        scratch_shapes=[pltpu.VMEM((2 * nb * n_pairs, PAIR, PAIR), F32)],
        compiler_params=_cparams(("parallel", "arbitrary")),
        name="wkv_scan",
    )(r, r, v, v, kk, kk, kd, kd, bb, bb, lw, lw, tri, m_strict, m_incl)


def _mix_out_kernel(y0_ref, y1_ref, gate_ref, bonus_ref, lw_ref, lb_ref, ones_ref, yb_ref,
                    wa_ref, wb_ref, g_ref, gt_ref, x_ref, o_ref):
    ones_bd = ones_ref[...]
    y = y0_ref[0] + y1_ref[0]
    inv_n = 1.0 / HEAD_DIM
    yc = y - _head_sum(y, ones_bd) * inv_n
    var = _head_sum(yc * yc, ones_bd) * inv_n
    yn = yc * lax.rsqrt(var + LNX_EPS) * lw_ref[...] + lb_ref[...]
    ya = (yn + bonus_ref[0].astype(F32)) * gate_ref[0].astype(F32)
    acc = _mm(ya, wa_ref[...]) + _mm(yb_ref[0], wb_ref[...])
    o_ref[0] = x_ref[0] + gt_ref[0] * _rmsnorm(acc, g_ref[...])


def _mix_out(y0, y1, gate, bonus, lnx_w, lnx_b, ones_bd, yb, w_out, g, gt, x, tm=512):
    bsz, s, d = x.shape
    width = y0.shape[-1]
    row = lambda w: pl.BlockSpec((1, tm, w), lambda b, i: (b, i, 0))
    vecs = lambda w: pl.BlockSpec((1, w), lambda b, i: (0, 0))
    wa, wb = w_out[:width].astype(BF16), w_out[width:].astype(BF16)
    return pl.pallas_call(
        _mix_out_kernel,
        grid=(bsz, s // tm),
        in_specs=[row(width)] * 4 + [vecs(width), vecs(width), _resident(ones_bd.shape),
                                     row(yb.shape[-1]), _resident(wa.shape), _resident(wb.shape),
                                     vecs(d), pl.BlockSpec((1, 1, d), lambda b, i: (b, 0, 0)), row(d)],
        out_specs=row(d),
        out_shape=jax.ShapeDtypeStruct((bsz, s, d), F32),
        compiler_params=_cparams(("parallel", "parallel")),
        name="rwkv_post_mix_out",
    )(y0, y1, gate, bonus, lnx_w.reshape(1, width), lnx_b.reshape(1, width), ones_bd, yb,
      wa, wb, g.reshape(1, d), gt.reshape(bsz, 1, d), x)


def _rope_rows(x, cos, sin_signed, swap_halves):
    partner = jnp.dot(x, swap_halves, preferred_element_type=F32)
    return x.astype(F32) * cos + partner * sin_signed


def _dilated_kernel(q_ref, k_ref, v_ref, cos_ref, sin_ref, swap_ref, perm_ref, o_ref,
                    qr, kp, vp, qg, kg, vg, m_s, den_s, num_s, *, seq, pad, rope_rows, big_dil):
    scale = HEAD_DIM ** -0.5
    half_w = QBLK // 2
    zeros_pad = jnp.zeros((pad, PAIR), F32)
    kp[0:pad] = zeros_pad
    vp[0:pad] = zeros_pad
    kp[pad + seq:pad + seq + pad] = zeros_pad
    vp[pad + seq:pad + seq + pad] = zeros_pad
    seg = seq // big_dil
    pitch = seg + 2 * half_w
    per = QBLK // big_dil
    zeros_hw = jnp.zeros((half_w, PAIR), F32)
    for res in range(big_dil):
        for buf in (kg, vg):
            buf[res * pitch:res * pitch + half_w] = zeros_hw
            buf[res * pitch + half_w + seg:(res + 1) * pitch] = zeros_hw

    def rope_body(j, carry):
        rows = pl.ds(pl.multiple_of(j * rope_rows, rope_rows), rope_rows)
        cos = cos_ref[rows, :]
        sin = sin_ref[rows, :]
        swap = swap_ref[...]
        q_rot = _rope_rows(q_ref[0, rows, :], cos, sin, swap) * scale
        k_rot = _rope_rows(k_ref[0, rows, :], cos, sin, swap)
        v_in = v_ref[0, rows, :]
        qr[rows, :] = q_rot
        dst = pl.ds(pl.multiple_of(pad + j * rope_rows, rope_rows), rope_rows)
        kp[dst, :] = k_rot
        vp[dst, :] = v_in.astype(F32)
        perm = perm_ref[...]
        for sb in range(rope_rows // QBLK):
            blk = j * (rope_rows // QBLK) + sb
            sl = slice(sb * QBLK, (sb + 1) * QBLK)
            for val, buf, stride, off in ((q_rot, qg, seg, 0), (k_rot, kg, pitch, half_w),
                                          (v_in, vg, pitch, half_w)):
                px = jnp.dot(perm, val[sl].astype(BF16), preferred_element_type=F32)
                for res in range(big_dil):
                    at = pl.ds(pl.multiple_of(res * stride + off + blk * per, per), per)
                    buf[at, :] = px[res * per:(res + 1) * per]
        return carry

    lax.fori_loop(0, seq // rope_rows, rope_body, 0)

    lane = lax.broadcasted_iota(jnp.int32, (1, PAIR), 1)
    head0 = lane < HEAD_DIM
    qi = lax.broadcasted_iota(jnp.int32, (QBLK, 1), 0)
    col = lax.broadcasted_iota(jnp.int32, (1, 2 * KBLK), 1)
    kj = col % KBLK
    left = col < KBLK
    band = jnp.abs(kj - half_w - qi) <= half_w
    ones_st = ((lax.broadcasted_iota(jnp.int32, (2 * KBLK, PAIR), 0) < KBLK)
               == (lax.broadcasted_iota(jnp.int32, (2 * KBLK, PAIR), 1) < HEAD_DIM)).astype(BF16)
    stack_rows = lambda x: jnp.concatenate([jnp.where(head0, x, 0.0), jnp.where(head0, 0.0, x)],
                                           axis=0).astype(BF16)

    for branch, (window, dil) in enumerate(sorted(DILATED_PAIRS, key=lambda wd: -wd[1])):
        assert window // (2 * dil) == half_w
        m_len = seq // dil
        blocks_per_res = m_len // QBLK

        def group_body(it, carry, dil=dil, m_len=m_len, blocks_per_res=blocks_per_res,
                       first=(branch == 0)):
            q_rows, qs, ks, vs, in_range = [], [], [], [], []
            for g in range(ATTN_GROUP):
                blk = it * ATTN_GROUP + g
                res = blk // blocks_per_res
                m0 = (blk % blocks_per_res) * QBLK
                q_start = res + dil * m0
                k_start = pad + res + dil * (m0 - half_w)
                if dil == 1:
                    rows = pl.ds(pl.multiple_of(q_start, QBLK), QBLK)
                    k_rows = pl.ds(pl.multiple_of(k_start, half_w), KBLK)
                else:
                    rows = pl.ds(q_start, QBLK, stride=dil)
                    k_rows = pl.ds(k_start, KBLK, stride=dil)
                q_rows.append(rows)
                if dil == big_dil:
                    g_rows = pl.ds(pl.multiple_of(res * pitch + m0, half_w), KBLK)
                    qb = qg[pl.ds(pl.multiple_of(res * seg + m0, QBLK), QBLK), :]
                    kb, vb = kg[g_rows, :], vg[g_rows, :]
                else:
                    qb, kb, vb = qr[rows, :], kp[k_rows, :], vp[k_rows, :]
                qs.append(qb.astype(BF16))
                ks.append(stack_rows(kb))
                vs.append(jnp.concatenate([stack_rows(vb), ones_st], axis=1))
                kpos = m0 - half_w + kj
                in_range.append((kpos >= 0) & (kpos < m_len))
            sc = _bmm(jnp.stack(qs), jnp.stack(ks), _BNT)
            sc = jnp.stack([jnp.where(in_range[g], jnp.where(band, sc[g], NEG_BIG), NEG_BIG)
                            for g in range(ATTN_GROUP)])
            tile_max = lambda x: jnp.maximum(x[:, :, :KBLK // 2], x[:, :, KBLK // 2:])
            mx0 = jnp.max(tile_max(sc[:, :, :KBLK]), axis=-1, keepdims=True)
            mx1 = jnp.max(tile_max(sc[:, :, KBLK:]), axis=-1, keepdims=True)
            p = jnp.exp(sc - jnp.where(left, mx0, mx1)).astype(BF16)
            nd = lax.dot_general(p, jnp.stack(vs), _BNN, preferred_element_type=F32)
            n_b = nd[:, :, :PAIR]
            d_b = nd[:, :, PAIR:]
            m_b = jnp.where(head0, mx0, mx1)
            for g, rows in enumerate(q_rows):
                if first:
                    m_s[rows, :] = m_b[g]
                    den_s[rows, :] = d_b[g]
                    num_s[rows, :] = n_b[g]
                else:
                    m_old = m_s[rows, :]
                    m_new = jnp.maximum(m_old, m_b[g])
                    alpha = jnp.exp(m_old - m_new)
                    beta = jnp.exp(m_b[g] - m_new)
                    m_s[rows, :] = m_new
                    den_s[rows, :] = alpha * den_s[rows, :] + beta * d_b[g]
                    num_s[rows, :] = alpha * num_s[rows, :] + beta * n_b[g]
            return carry

        lax.fori_loop(0, seq // (QBLK * ATTN_GROUP), group_body, 0)

    def out_body(j, carry):
        rows = pl.ds(pl.multiple_of(j * rope_rows, rope_rows), rope_rows)
        o_ref[0, rows, :] = (num_s[rows, :] / den_s[rows, :]).astype(o_ref.dtype)
        return carry

    lax.fori_loop(0, seq // rope_rows, out_body, 0)


def _dilated_attention(z, col0, width):
    bsz, s, _ = z.shape
    n_pairs = width // PAIR
    max_dil = max(d for _, d in DILATED_PAIRS)
    assert s % (QBLK * max_dil) == 0 and s % (QBLK * ATTN_GROUP) == 0 and col0 % PAIR == 0
    pad = (QBLK // 2) * max_dil
    half = HEAD_DIM // 2
    inv = ROPE_THETA ** (-jnp.arange(half, dtype=F32) / half)
    ang = jnp.arange(s, dtype=F32)[:, None] * inv[None, :]
    cos = jnp.tile(jnp.cos(ang), (1, PAIR // half))
    sin = jnp.tile(jnp.concatenate([-jnp.sin(ang), jnp.sin(ang)], axis=1), (1, 2))
    per = QBLK // max_dil
    row = np.arange(QBLK)
    perm = jnp.asarray(row[None, :] == (row[:, None] % per) * max_dil + row[:, None] // per, BF16)
    lane = np.arange(PAIR)
    partner_lane = np.where(lane % HEAD_DIM < half, lane + half, lane - half)
    swap = jnp.asarray(lane[:, None] == partner_lane[None, :], BF16)
    assert z.dtype == BF16
    cb = col0 // PAIR
    col = lambda off: pl.BlockSpec((1, s, PAIR), lambda b, p: (b, 0, cb + off * n_pairs + p))
    rope_rows = 512
    return pl.pallas_call(
        functools.partial(_dilated_kernel, seq=s, pad=pad, rope_rows=rope_rows, big_dil=max_dil),
        grid=(bsz, n_pairs),
        in_specs=[col(0), col(1), col(2), _resident((s, PAIR)), _resident((s, PAIR)),
                  _resident((PAIR, PAIR)), _resident((QBLK, QBLK))],
        out_specs=pl.BlockSpec((1, s, PAIR), lambda b, p: (b, 0, p)),
        out_shape=jax.ShapeDtypeStruct((bsz, s, width), BF16),
        scratch_shapes=[pltpu.VMEM((s, PAIR), F32),
                        pltpu.VMEM((s + 2 * pad, PAIR), F32),
                        pltpu.VMEM((s + 2 * pad, PAIR), F32),
                        pltpu.VMEM((s, PAIR), F32),
                        pltpu.VMEM((s + QBLK * max_dil, PAIR), F32),
                        pltpu.VMEM((s + QBLK * max_dil, PAIR), F32),
                        pltpu.VMEM((s, PAIR), F32),
                        pltpu.VMEM((s, PAIR), F32),
                        pltpu.VMEM((s, PAIR), F32)],
        compiler_params=_cparams(("parallel", "parallel")),
        name="dilated_attention",
    )(z, z, z, cos, sin, swap, perm)


def _seq_dft_kernel(xc_ref, xs_ref, cm_ref, sm_ref, jm_ref, alt_ref, o_ref, xe_s, yo_s,
                    *, seq, t):
    h = seq // 2
    nb = h // t
    scale = seq ** -0.5
    jm = jm_ref[...]
    row0 = lax.broadcasted_iota(jnp.int32, (t, 1), 0) == 0
    for i in range(nb):
        rows = slice(i * t, (i + 1) * t)
        mirror = slice(seq - (i + 1) * t, seq - i * t)
        pc = jnp.dot(jm, xc_ref[0, mirror, :], preferred_element_type=F32)
        ps = jnp.dot(jm, xs_ref[0, mirror, :], preferred_element_type=F32)
        if i > 0:
            first = slice(seq - i * t, seq - i * t + 1)
            pc = jnp.where(row0, xc_ref[0, first, :].astype(F32), pc)
            ps = jnp.where(row0, xs_ref[0, first, :].astype(F32), ps)
        xe_s[rows, :] = (xc_ref[0, rows, :].astype(F32) + pc).astype(BF16)
        yo_s[rows, :] = (xs_ref[0, rows, :].astype(F32) - ps).astype(BF16)
    xe = xe_s[...]
    p = jnp.dot(cm_ref[...], xe, preferred_element_type=F32)
    q = jnp.dot(sm_ref[...], yo_s[...], preferred_element_type=F32)
    alt = alt_ref[...]
    x_h = xc_ref[0, h:h + 1, :].astype(F32) * scale
    k_odd = lax.broadcasted_iota(jnp.int32, (h, 1), 0) % 2 == 1
    alt_x = jnp.where(k_odd, -x_h, x_h)
    o_ref[0, 0:h, :] = (p - q + alt_x).astype(o_ref.dtype)
    pq = (p + q + alt_x).astype(BF16)
    f_h = scale * jnp.dot(alt, xe, preferred_element_type=F32)[0:1] + x_h
    for i in range(nb):
        rev = jnp.dot(jm, pq[h - (i + 1) * t:h - i * t], preferred_element_type=F32)
        first = f_h if i == 0 else pq[h - i * t:h - i * t + 1].astype(F32)
        o_ref[0, h + i * t:h + (i + 1) * t, :] = jnp.where(row0, first, rev).astype(o_ref.dtype)


def _seq_dft(cm, sm, x2, tn=256, t=128):
    bsz, s, two_d = x2.shape
    d = two_d // 2
    h = s // 2
    assert h % 2 == 0 and h % t == 0 and d % tn == 0
    r = np.arange(t)
    jm = jnp.asarray((r[None, :] == t - r[:, None]) & (r[:, None] >= 1), BF16)
    alt = jnp.asarray(np.where(np.arange(8)[:, None] == 0, 1.0 - 2.0 * (np.arange(h) % 2), 0.0), BF16)
    nj = d // tn
    return pl.pallas_call(
        functools.partial(_seq_dft_kernel, seq=s, t=t),
        grid=(bsz, nj),
        in_specs=[pl.BlockSpec((1, s, tn), lambda b, j: (b, 0, j)),
                  pl.BlockSpec((1, s, tn), lambda b, j: (b, 0, nj + j)),
                  _resident((h, h)), _resident((h, h)), _resident((t, t)), _resident((8, h))],
        out_specs=pl.BlockSpec((1, s, tn), lambda b, j: (b, 0, j)),
        out_shape=jax.ShapeDtypeStruct((bsz, s, d), BF16),
        scratch_shapes=[pltpu.VMEM((h, tn), BF16), pltpu.VMEM((h, tn), BF16)],
        compiler_params=_cparams(("parallel", "parallel")),
        name="seq_dft",
    )(x2, x2, cm, sm, jm, alt)


def _dft_tables(s, d):
    gsz = d // N_FOURIER_GROUPS
    cidx = jnp.arange(gsz, dtype=jnp.int32)
    ang_c = (2.0 * np.pi / gsz) * ((cidx[:, None] * cidx[None, :]) % gsz).astype(F32)
    w_ch = jnp.concatenate([jnp.cos(ang_c), jnp.sin(ang_c)], axis=1) * (gsz ** -0.5)
    q = 64
    h = s // 2
    assert h % q == 0
    sidx = jnp.arange(h, dtype=jnp.int32)[None, :]
    ang = lambda kvals: (2.0 * np.pi / s) * ((kvals[:, None] * sidx) % s).astype(F32)
    ang_hi = ang(q * jnp.arange(h // q, dtype=jnp.int32))[:, None, :]
    ang_lo = ang(jnp.arange(q, dtype=jnp.int32))[None, :, :]
    scale = s ** -0.5
    ch, sh, cl, sl = jnp.cos(ang_hi) * scale, jnp.sin(ang_hi) * scale, jnp.cos(ang_lo), jnp.sin(ang_lo)
    cm = (ch * cl - sh * sl).reshape(h, h).astype(BF16)
    sm = (sh * cl + ch * sl).reshape(h, h).astype(BF16)
    return w_ch.astype(BF16), cm, sm


def kernel(x, c, ada_w, ada_b, norm_g, mix_in, mix_conv, rwkv_w0, rwkv_w1, rwkv_w2, rwkv_a0, rwkv_a1, rwkv_a2, rwkv_g1, rwkv_g2, rwkv_k_k, rwkv_k_a, rwkv_r_k, rwkv_lnx_w, rwkv_lnx_b, mix_out, fnet_w, ffn_up, ffn_conv, ffn_down):
    bsz, s, d = x.shape
    depth = ada_w.shape[0]
    width_a = rwkv_k_k.shape[-1]
    width_b = mix_out.shape[1] - width_a
    in_cols_a = 4 * width_a
    ones_bd = jnp.kron(jnp.eye(width_a // HEAD_DIM, dtype=F32),
                       jnp.ones((HEAD_DIM, HEAD_DIM), F32)).astype(BF16)

    mod = _adaln_mod(c, ada_w, ada_b)
    for l in range(depth):
        sh1, sc1, gt1, sh2, sc2, gt2 = [mod[l, :, j * d:(j + 1) * d] for j in range(6)]
        if l % 2 == 0:
            e = l // 2
            zb, r, v, kk, g, bonus, kd, bb, lw = _mix_prep(
                x, norm_g[l, 0], sc1, sh1, mix_in[e], mix_conv[e], rwkv_w0[e], rwkv_w1[e],
                rwkv_w2[e], rwkv_a0[e], rwkv_a1[e], rwkv_a2[e], rwkv_g1[e], rwkv_g2[e],
                rwkv_k_k[e], rwkv_k_a[e], rwkv_r_k[e], ones_bd)
            y0, y1 = _wkv_scan(r, v, kk, kd, bb, lw)
            yb = _dilated_attention(zb, 0, width_b)
            x = _mix_out(y0, y1, g, bonus, rwkv_lnx_w[e], rwkv_lnx_b[e], ones_bd, yb,
                         mix_out[e], norm_g[l, 1], gt1, x)
        else:
            w_ch, cm, sm = _dft_tables(s, d)
            x2 = _chan_dft(x, norm_g[l, 0], sc1, sh1, w_ch)
            f = _seq_dft(cm, sm, x2)
            x = _out_proj([f], [fnet_w[l // 2]], norm_g[l, 1], gt1, x)
        x = _ffn(x, norm_g[l, 2], sc2, sh2, ffn_up[l], ffn_conv[l], ffn_down[l],
                 norm_g[l, 3], gt2)
    return x
```

```python
import functools

import jax
import jax.numpy as jnp
import numpy as np
from jax import lax
from jax.experimental import pallas as pl
from jax.experimental.pallas import tpu as pltpu

F32 = jnp.float32
BF16 = jnp.bfloat16

HEAD_DIM = 64
DILATED_PAIRS = ((128, 1), (512, 4), (2048, 16))
ROPE_THETA = 10000.0
N_FOURIER_GROUPS = 8
RMS_EPS = 1e-6
LNX_EPS = 64e-5
L2_EPS = 1e-12
NEG_BIG = -1e30

LANES = 128
SUBLANES = 8
VMEM_LIMIT_BYTES = 56 * 1024 * 1024

PAIR = 2 * HEAD_DIM
assert PAIR == LANES
HALO = SUBLANES
CHUNK = 64
QBLK = 128
KBLK = 2 * QBLK
ATTN_GROUP = 4


def _cparams(sem):
    return pltpu.CompilerParams(dimension_semantics=sem, vmem_limit_bytes=VMEM_LIMIT_BYTES)


def _resident(shape):
    zeros = (0,) * len(shape)
    return pl.BlockSpec(shape, lambda *_: zeros, pipeline_mode=pl.Buffered(1))


def _mm(a, b):
    return jnp.dot(a.astype(BF16), b.astype(BF16), preferred_element_type=F32)


def _mm_nt(a, b):
    return lax.dot_general(a.astype(BF16), b.astype(BF16), (((1,), (1,)), ((), ())),
                           preferred_element_type=F32)


def _split(x, n):
    parts = []
    rem = x
    for _ in range(n):
        p = rem.astype(BF16)
        parts.append(p)
        rem = rem - p.astype(F32)
    return parts


def _rmsnorm(x, g):
    return x * lax.rsqrt(jnp.mean(x * x, -1, keepdims=True) + RMS_EPS) * g


def _conv3_rows(z, w, n_rows):
    total = z.shape[0]
    y = pltpu.roll(z, 1, axis=0) * w[0:1] + z * w[1:2] + pltpu.roll(z, total - 1, axis=0) * w[2:3]
    return y[HALO:HALO + n_rows]


def _halo_specs(tm, width, seq, col_block=0):
    per = tm // HALO
    last = seq // HALO - 1

    def prev_map(b, i):
        return (b, jnp.maximum(i * per - 1, 0), col_block)

    def next_map(b, i):
        return (b, jnp.minimum((i + 1) * per, last), col_block)

    return [pl.BlockSpec((1, HALO, width), prev_map),
            pl.BlockSpec((1, tm, width), lambda b, i: (b, i, col_block)),
            pl.BlockSpec((1, HALO, width), next_map)]


def _mod_kernel(c_ref, w_ref, b_ref, o_ref):
    c = c_ref[...]
    cs = c * jax.nn.sigmoid(c)
    o_ref[0] = jnp.dot(cs, w_ref[0], precision=lax.Precision.HIGHEST,
                       preferred_element_type=F32) + b_ref[0]


def _adaln_mod(c, ada_w, ada_b):
    depth, d, n = ada_w.shape
    b = c.shape[0]
    tn = n // 4
    return pl.pallas_call(
        _mod_kernel,
        grid=(depth, n // tn),
        in_specs=[pl.BlockSpec((b, d), lambda l, j: (0, 0)),
                  pl.BlockSpec((1, d, tn), lambda l, j: (l, 0, j)),
                  pl.BlockSpec((1, 1, tn), lambda l, j: (l, 0, j))],
        out_specs=pl.BlockSpec((1, b, tn), lambda l, j: (l, 0, j)),
        out_shape=jax.ShapeDtypeStruct((depth, b, n), F32),
        compiler_params=_cparams(("parallel", "parallel")),
        name="adaln_mod",
    )(c, ada_w, ada_b.reshape(depth, 1, n))


def _chan_dft_kernel(x_ref, g_ref, sc_ref, sh_ref, w_ref, o_ref):
    h = (_rmsnorm(x_ref[0], g_ref[...]) * (1.0 + sc_ref[0]) + sh_ref[0]).astype(BF16)
    d = h.shape[1]
    gsz = w_ref.shape[0]
    w = w_ref[...]
    for grp in range(d // gsz):
        cols = slice(grp * gsz, (grp + 1) * gsz)
        xy = jnp.dot(h[:, cols], w, preferred_element_type=F32)
        o_ref[0, :, cols] = xy[:, :gsz].astype(o_ref.dtype)
        o_ref[0, :, d + grp * gsz:d + (grp + 1) * gsz] = xy[:, gsz:].astype(o_ref.dtype)


def _chan_dft(x, g, sc, sh, w, tm=1024):
    bsz, s, d = x.shape
    assert w.shape[0] % LANES == 0 and d % w.shape[0] == 0
    vec = pl.BlockSpec((1, 1, d), lambda b, i: (b, 0, 0))
    return pl.pallas_call(
        _chan_dft_kernel,
        grid=(bsz, s // tm),
        in_specs=[pl.BlockSpec((1, tm, d), lambda b, i: (b, i, 0)),
                  pl.BlockSpec((1, d), lambda b, i: (0, 0)), vec, vec,
                  _resident(w.shape)],
        out_specs=pl.BlockSpec((1, tm, 2 * d), lambda b, i: (b, i, 0)),
        out_shape=jax.ShapeDtypeStruct((bsz, s, 2 * d), BF16),
        compiler_params=_cparams(("parallel", "parallel")),
        name="chan_dft",
    )(x, g.reshape(1, d), sc.reshape(bsz, 1, d), sh.reshape(bsz, 1, d), w)


def _out_proj_kernel(*refs, n_in):
    y_refs = refs[:n_in]
    w_refs = refs[n_in:2 * n_in]
    g_ref, gt_ref, x_ref, o_ref = refs[2 * n_in:]
    acc = _mm(y_refs[0][0], w_refs[0][...])
    for y_ref, w_ref in zip(y_refs[1:], w_refs[1:]):
        acc = acc + _mm(y_ref[0], w_ref[...])
    o_ref[0] = x_ref[0] + gt_ref[0] * _rmsnorm(acc, g_ref[...])


def _out_proj(ys, ws, g, gt, x, tm=1024):
    bsz, s, d = x.shape
    n_in = len(ys)
    row = lambda width: pl.BlockSpec((1, tm, width), lambda b, i: (b, i, 0))
    return pl.pallas_call(
        functools.partial(_out_proj_kernel, n_in=n_in),
        grid=(bsz, s // tm),
        in_specs=([row(y.shape[-1]) for y in ys] + [_resident(w.shape) for w in ws]
                  + [pl.BlockSpec((1, d), lambda b, i: (0, 0)),
                     pl.BlockSpec((1, 1, d), lambda b, i: (b, 0, 0)), row(d)]),
        out_specs=row(d),
        out_shape=jax.ShapeDtypeStruct((bsz, s, d), F32),
        compiler_params=_cparams(("parallel", "parallel")),
        name="out_proj",
    )(*ys, *[w.astype(BF16) for w in ws], g.reshape(1, d), gt.reshape(bsz, 1, d), x)


def _ffn_kernel(xp_ref, x_ref, xn_ref, g0_ref, sc_ref, sh_ref, up_ref, cw_ref, dn_ref,
                g1_ref, gt_ref, o_ref, *, tm, d_ff, n_chunks):
    i = pl.program_id(1)
    last = pl.num_programs(1) - 1
    x = x_ref[0]
    xh = jnp.concatenate([xp_ref[0], x, xn_ref[0]], axis=0)
    h = _rmsnorm(xh, g0_ref[...]) * (1.0 + sc_ref[0]) + sh_ref[0]
    rows = lax.broadcasted_iota(jnp.int32, (tm + 2 * HALO, 1), 0)
    lo = jnp.where(i > 0, 0, HALO)
    hi = jnp.where(i < last, tm + 2 * HALO, tm + HALO)
    h = jnp.where((rows >= lo) & (rows < hi), h, 0.0).astype(BF16)
    fc = d_ff // n_chunks
    acc = jnp.zeros((tm, x.shape[1]), F32)
    for f in range(n_chunks):
        gsl = slice(f * fc, (f + 1) * fc)
        vsl = slice(d_ff + f * fc, d_ff + (f + 1) * fc)
        zg = _conv3_rows(_mm(h, up_ref[:, gsl]), cw_ref[:, gsl], tm)
        zv = _conv3_rows(_mm(h, up_ref[:, vsl]), cw_ref[:, vsl], tm)
        act = jax.nn.gelu(zg, approximate=True) * zv
        acc = acc + _mm(act, dn_ref[gsl, :])
    o_ref[0] = x + gt_ref[0] * _rmsnorm(acc, g1_ref[...])


def _ffn(x, g0, sc, sh, up, conv, down, g1, gt, tm=512, n_chunks=2):
    bsz, s, d = x.shape
    d_ff = down.shape[0]
    vec = pl.BlockSpec((1, 1, d), lambda b, i: (b, 0, 0))
    gvec = pl.BlockSpec((1, d), lambda b, i: (0, 0))
    return pl.pallas_call(
        functools.partial(_ffn_kernel, tm=tm, d_ff=d_ff, n_chunks=n_chunks),
        grid=(bsz, s // tm),
        in_specs=(_halo_specs(tm, d, s)
                  + [gvec, vec, vec, _resident(up.shape), _resident(conv.shape),
                     _resident(down.shape), gvec, vec]),
        out_specs=pl.BlockSpec((1, tm, d), lambda b, i: (b, i, 0)),
        out_shape=jax.ShapeDtypeStruct((bsz, s, d), F32),
        compiler_params=_cparams(("parallel", "parallel")),
        name="conv_glu_ffn",
    )(x, x, x, g0.reshape(1, d), sc.reshape(bsz, 1, d), sh.reshape(bsz, 1, d),
      up.astype(BF16), conv, down.astype(BF16), g1.reshape(1, d), gt.reshape(bsz, 1, d))


def _head_sum(x, ones_bd, pieces=2):
    return sum(jnp.dot(p, ones_bd, preferred_element_type=F32) for p in _split(x, pieces))


def _mix_prep_kernel(xp_ref, x_ref, xn_ref, g0_ref, sc_ref, sh_ref, win_ref, cw_ref, w1_ref,
                     w2_ref, vec_ref, ones_ref, zb_o, r_o, v_o, kk_o, g_o, bonus_o, kd_o, b_o,
                     lw_o, *, tm, sub, width):
    i = pl.program_id(1)
    last = pl.num_programs(1) - 1
    xh = jnp.concatenate([xp_ref[0], x_ref[0], xn_ref[0]], axis=0)
    h = _rmsnorm(xh, g0_ref[...]) * (1.0 + sc_ref[0]) + sh_ref[0]
    rows = lax.broadcasted_iota(jnp.int32, (tm + 2 * HALO, 1), 0)
    lo = jnp.where(i > 0, 0, HALO)
    hi = jnp.where(i < last, tm + 2 * HALO, tm + HALO)
    h = jnp.where((rows >= lo) & (rows < hi), h, 0.0).astype(BF16)
    cols_a = 4 * width
    ones_bd = ones_ref[...]
    vec = vec_ref[...]
    k_k, k_a, r_k = vec[4:5], vec[5:6], vec[6:7]
    for j in range(tm // sub):
        out = slice(j * sub, (j + 1) * sub)
        hs = h[j * sub:(j + 1) * sub + 2 * HALO]
        za = _conv3_rows(_mm(hs, win_ref[:, :cols_a]), cw_ref[...], sub)
        r = za[:, 0:width]
        k = za[:, width:2 * width]
        v = za[:, 2 * width:3 * width]
        u = za[:, 3 * width:4 * width]

        lora = _mm(u, w1_ref[...])
        g_o[0, out] = _mm(jax.nn.sigmoid(lora[:, 4 * LANES:5 * LANES]), w2_ref[4]).astype(g_o.dtype)
        kk = k * k_k
        kk = kk * lax.rsqrt(_head_sum(kk * kk, ones_bd) + L2_EPS)
        r_o[0, out] = r.astype(r_o.dtype)
        v_o[0, out] = v.astype(v_o.dtype)
        kk_o[0, out] = kk.astype(kk_o.dtype)
        kd_sum = jnp.zeros_like(r)
        for d in range(2):
            q = vec[d:d + 1] + _mm(jnp.tanh(lora[:, d * LANES:(d + 1) * LANES]), w2_ref[d])
            w_log = jnp.minimum(q, 0.0) - jnp.log1p(jnp.exp(-jnp.abs(q))) - 0.5
            a = jax.nn.sigmoid(vec[2 + d:3 + d]
                               + _mm(lora[:, (2 + d) * LANES:(3 + d) * LANES], w2_ref[2 + d]))
            kd = k * (1.0 + (a - 1.0) * k_a)
            kd_o[d, 0, out] = kd.astype(kd_o.dtype)
            b_o[d, 0, out] = (kk * a).astype(b_o.dtype)
            lw_o[d, 0, out] = -jnp.exp(w_log)
            kd_sum = kd_sum + kd
        bonus = _head_sum(r * r_k * kd_sum, ones_bd, pieces=1)
        bonus_o[0, out] = (bonus * v).astype(bonus_o.dtype)
        zb_o[0, out] = _mm(hs[HALO:HALO + sub], win_ref[:, cols_a:]).astype(zb_o.dtype)


def _mix_prep(x, g0, sc, sh, w_in, conv, w0, w1, w2, a0, a1, a2, g1, g2, k_k, k_a, r_k, ones_bd,
              tm=512, sub=256):
    bsz, s, d = x.shape
    width = k_k.shape[0]
    lora = w1.shape[-1]
    cols_b = w_in.shape[1] - 4 * width
    pad_c = lambda m: jnp.pad(m, ((0, 0), (0, LANES - m.shape[1])))
    pad_r = lambda m: jnp.pad(m, ((0, LANES - m.shape[0]), (0, 0)))
    assert lora <= LANES and g1.shape[1] == LANES
    w1cat = jnp.concatenate([pad_c(w1[0]), pad_c(w1[1]), pad_c(a1[0]), pad_c(a1[1]), g1],
                            axis=1).astype(BF16)
    w2cat = jnp.stack([pad_r(w2[0]), pad_r(w2[1]), pad_r(a2[0]), pad_r(a2[1]), g2]).astype(BF16)
    vec = jnp.stack([w0[0], w0[1], a0[0], a0[1], k_k, k_a, r_k.reshape(-1),
                     jnp.zeros_like(k_k)])
    out1 = jax.ShapeDtypeStruct((bsz, s, width), BF16)
    out2 = jax.ShapeDtypeStruct((2, bsz, s, width), BF16)
    spec1 = pl.BlockSpec((1, tm, width), lambda b, i: (b, i, 0))
    spec2 = pl.BlockSpec((2, 1, tm, width), lambda b, i: (0, b, i, 0))
    mvec = pl.BlockSpec((1, 1, d), lambda b, i: (b, 0, 0))
    return pl.pallas_call(
        functools.partial(_mix_prep_kernel, tm=tm, sub=sub, width=width),
        grid=(bsz, s // tm),
        in_specs=(_halo_specs(tm, d, s)
                  + [pl.BlockSpec((1, d), lambda b, i: (0, 0)), mvec, mvec,
                     _resident(w_in.shape), _resident(conv.shape), _resident(w1cat.shape),
                     _resident(w2cat.shape), _resident(vec.shape), _resident(ones_bd.shape)]),
        out_specs=([pl.BlockSpec((1, tm, cols_b), lambda b, i: (b, i, 0))]
                   + [spec1] * 5 + [spec2] * 3),
        out_shape=([jax.ShapeDtypeStruct((bsz, s, cols_b), BF16)] + [out1] * 5 + [out2] * 2
                   + [jax.ShapeDtypeStruct((2, bsz, s, width), F32)]),
        compiler_params=_cparams(("parallel", "parallel")),
        name="mix_in_rwkv_prep",
    )(x, x, x, g0.reshape(1, d), sc.reshape(bsz, 1, d), sh.reshape(bsz, 1, d),
      w_in.astype(BF16), conv, w1cat, w2cat, vec, ones_bd)


_BNN = (((2,), (1,)), ((0,), (0,)))
_BNT = (((2,), (2,)), ((0,), (0,)))
_BTN = (((1,), (1,)), ((0,), (0,)))


def _bmm(a, b, dims=_BNN):
    return lax.dot_general(a.astype(BF16), b.astype(BF16), dims, preferred_element_type=F32)


def _stack(x, head0):
    return jnp.concatenate([jnp.where(head0, x, 0.0), jnp.where(head0, 0.0, x)], axis=1)


def _wkv_group(r, k, v, kk, b, lw, ht0, tri, m_strict, m_incl):
    _, c, _ = r.shape
    n2 = 2 * c
    lw3 = _split(lw, 3)
    cum = sum(lax.dot_general(tri, p, _BNN, preferred_element_type=F32) for p in lw3)
    tot = jnp.sum(lw, axis=1, keepdims=True)
    g_inv = jnp.exp(-cum)
    g_rat = jnp.exp(tot - cum)
    lane = lax.broadcasted_iota(jnp.int32, (1, 1, PAIR), 2)
    head0 = lane < HEAD_DIM
    left = lane < c
    kt = kk * jnp.exp(cum - lw)
    rt = r * jnp.exp(cum)
    vs = _stack(v, head0)

    a_all = _bmm(jnp.concatenate([kt, rt], axis=1),
                 jnp.concatenate([_stack(k * g_inv, head0), _stack(b * g_inv, head0)], axis=1),
                 _BNT)
    akk = jnp.where(m_strict, a_all[:, :c, :n2], 0.0)
    akb = jnp.where(m_strict, a_all[:, :c, n2:], 0.0)
    ark = jnp.where(m_incl, a_all[:, c:, :n2], 0.0)
    arb = jnp.where(m_incl, a_all[:, c:, n2:], 0.0)

    eye = (lax.broadcasted_iota(jnp.int32, (1, c, n2), 1)
           == lax.broadcasted_iota(jnp.int32, (1, c, n2), 2) % c).astype(F32)
    n = -akb
    t = eye + n
    p = _bmm(n, _stack(n, left))
    for _ in range(int(np.ceil(np.log2(c))) - 2):
        pt = _bmm(jnp.concatenate([p, t], axis=1), _stack(p, left))
        p = pt[:, :c]
        t = t + pt[:, c:]
    t = t + _bmm(t, _stack(p, left))
    t_hi, t_lo = _split(t, 2)
    ia_t = lax.dot_general((eye + akb).astype(BF16),
                           jnp.concatenate([_stack(t_hi, left), _stack(t_lo, left)], axis=2),
                           _BNN, preferred_element_type=F32)
    t = t + _bmm(t, _stack(eye - ia_t[:, :, :n2] - ia_t[:, :, n2:], left))

    hk = _bmm(jnp.concatenate([kt, rt], axis=1), ht0, _BNT)
    av = _bmm(jnp.concatenate([akk, ark], axis=1), vs)
    u = _bmm(t, _stack(hk[:, :c] + av[:, :c], head0))
    y = hk[:, c:] + av[:, c:] - _bmm(arb, _stack(u, head0))
    same_head = ((lax.broadcasted_iota(jnp.int32, (1, PAIR, PAIR), 1) < HEAD_DIM)
                 == (lax.broadcasted_iota(jnp.int32, (1, PAIR, PAIR), 2) < HEAD_DIM))
    upd_t = _bmm(jnp.concatenate([v, u], axis=1),
                 jnp.concatenate([k * g_rat, -(b * g_rat)], axis=1), _BTN)
    ht_new = jnp.exp(tot) * ht0 + jnp.where(same_head, upd_t, 0.0)
    return y, ht_new


def _wkv_kernel(r0_ref, r1_ref, v0_ref, v1_ref, kk0_ref, kk1_ref, k0_ref, k1_ref, b0_ref, b1_ref,
                lw0_ref, lw1_ref, tri_ref, ms_ref, mi_ref, y0_ref, y1_ref, h_ref, *, n_pairs, nb):
    @pl.when(pl.program_id(1) == 0)
    def _():
        h_ref[...] = jnp.zeros_like(h_ref)

    cells = [(bi, slice(p * PAIR, (p + 1) * PAIR)) for bi in range(nb) for p in range(n_pairs)]

    def group(ref0, ref1):
        return jnp.stack([ref0[bi, :, sl] for bi, sl in cells]
                         + [ref1[bi, :, sl] for bi, sl in cells]).astype(F32)

    per_dir = lambda ref: jnp.stack([ref[0]] * len(cells) + [ref[1]] * len(cells))
    y, h_new = _wkv_group(group(r0_ref, r1_ref), group(k0_ref, k1_ref), group(v0_ref, v1_ref),
                          group(kk0_ref, kk1_ref), group(b0_ref, b1_ref), group(lw0_ref, lw1_ref),
                          h_ref[...], per_dir(tri_ref), per_dir(ms_ref) > 0.5, per_dir(mi_ref) > 0.5)
    h_ref[...] = h_new
    for i, (bi, sl) in enumerate(cells):
        y0_ref[bi, :, sl] = y[i]
        y1_ref[bi, :, sl] = y[len(cells) + i]


def _wkv_scan(r, v, kk, kd, bb, lw):
    bsz, s, width = r.shape
    c = CHUNK
    nc = s // c
    n_pairs = width // PAIR
    t_idx = np.arange(c)
    before = np.stack([t_idx[None, :] < t_idx[:, None], t_idx[None, :] > t_idx[:, None]])
    eye = np.eye(c, dtype=bool)[None]
    tri = jnp.asarray(before | eye, BF16)
    m_strict = jnp.asarray(np.tile(before, (1, 1, 2)), F32)
    m_incl = jnp.asarray(np.tile(before | eye, (1, 1, 2)), F32)

    nb = max(n for n in (4, 2, 1) if bsz % n == 0)
    fwd = pl.BlockSpec((nb, c, width), lambda b, ci: (b, ci, 0))
    bwd = pl.BlockSpec((nb, c, width), lambda b, ci: (b, nc - 1 - ci, 0))
    fwd_d = pl.BlockSpec((None, nb, c, width), lambda b, ci: (0, b, ci, 0))
    bwd_d = pl.BlockSpec((None, nb, c, width), lambda b, ci: (1, b, nc - 1 - ci, 0))
    return pl.pallas_call(
        functools.partial(_wkv_kernel, n_pairs=n_pairs, nb=nb),
        grid=(bsz // nb, nc),
        in_specs=[fwd, bwd] * 3 + [fwd_d, bwd_d] * 3 + [
            _resident(tri.shape), _resident(m_strict.shape), _resident(m_incl.shape)],
        out_specs=[fwd, bwd],
        out_shape=[jax.ShapeDtypeStruct((bsz, s, width), F32)] * 2,
        scratch_shapes=[pltpu.VMEM((2 * nb * n_pairs, PAIR, PAIR), F32)],
        compiler_params=_cparams(("parallel", "arbitrary")),
        name="wkv_scan",
    )(r, r, v, v, kk, kk, kd, kd, bb, bb, lw, lw, tri, m_strict, m_incl)


def _mix_out_kernel(y0_ref, y1_ref, gate_ref, bonus_ref, lw_ref, lb_ref, ones_ref, yb_ref,
                    wa_ref, wb_ref, g_ref, gt_ref, x_ref, o_ref):
    ones_bd = ones_ref[...]
    y = y0_ref[0] + y1_ref[0]
    inv_n = 1.0 / HEAD_DIM
    yc = y - _head_sum(y, ones_bd) * inv_n
    var = _head_sum(yc * yc, ones_bd) * inv_n
    yn = yc * lax.rsqrt(var + LNX_EPS) * lw_ref[...] + lb_ref[...]
    ya = (yn + bonus_ref[0].astype(F32)) * gate_ref[0].astype(F32)
    acc = _mm(ya, wa_ref[...]) + _mm(yb_ref[0], wb_ref[...])
    o_ref[0] = x_ref[0] + gt_ref[0] * _rmsnorm(acc, g_ref[...])


def _mix_out(y0, y1, gate, bonus, lnx_w, lnx_b, ones_bd, yb, w_out, g, gt, x, tm=512):
    bsz, s, d = x.shape
    width = y0.shape[-1]
    row = lambda w: pl.BlockSpec((1, tm, w), lambda b, i: (b, i, 0))
    vecs = lambda w: pl.BlockSpec((1, w), lambda b, i: (0, 0))
    wa, wb = w_out[:width].astype(BF16), w_out[width:].astype(BF16)
    return pl.pallas_call(
        _mix_out_kernel,
        grid=(bsz, s // tm),
        in_specs=[row(width)] * 4 + [vecs(width), vecs(width), _resident(ones_bd.shape),
                                     row(yb.shape[-1]), _resident(wa.shape), _resident(wb.shape),
                                     vecs(d), pl.BlockSpec((1, 1, d), lambda b, i: (b, 0, 0)), row(d)],
        out_specs=row(d),
        out_shape=jax.ShapeDtypeStruct((bsz, s, d), F32),
        compiler_params=_cparams(("parallel", "parallel")),
        name="rwkv_post_mix_out",
    )(y0, y1, gate, bonus, lnx_w.reshape(1, width), lnx_b.reshape(1, width), ones_bd, yb,
      wa, wb, g.reshape(1, d), gt.reshape(bsz, 1, d), x)


def _rope_rows(x, cos, sin_signed, swap_halves):
    partner = jnp.dot(x, swap_halves, preferred_element_type=F32)
    return x.astype(F32) * cos + partner * sin_signed


def _dilated_kernel(q_ref, k_ref, v_ref, cos_ref, sin_ref, swap_ref, perm_ref, o_ref,
                    qr, kp, vp, qg, kg, vg, m_s, den_s, num_s, *, seq, pad, rope_rows, big_dil):
    scale = HEAD_DIM ** -0.5
    half_w = QBLK // 2
    zeros_pad = jnp.zeros((pad, PAIR), F32)
    kp[0:pad] = zeros_pad
    vp[0:pad] = zeros_pad
    kp[pad + seq:pad + seq + pad] = zeros_pad
    vp[pad + seq:pad + seq + pad] = zeros_pad
    seg = seq // big_dil
    pitch = seg + 2 * half_w
    per = QBLK // big_dil
    zeros_hw = jnp.zeros((half_w, PAIR), F32)
    for res in range(big_dil):
        for buf in (kg, vg):
            buf[res * pitch:res * pitch + half_w] = zeros_hw
            buf[res * pitch + half_w + seg:(res + 1) * pitch] = zeros_hw

    def rope_body(j, carry):
        rows = pl.ds(pl.multiple_of(j * rope_rows, rope_rows), rope_rows)
        cos = cos_ref[rows, :]
        sin = sin_ref[rows, :]
        swap = swap_ref[...]
        q_rot = _rope_rows(q_ref[0, rows, :], cos, sin, swap) * scale
        k_rot = _rope_rows(k_ref[0, rows, :], cos, sin, swap)
        v_in = v_ref[0, rows, :]
        qr[rows, :] = q_rot
        dst = pl.ds(pl.multiple_of(pad + j * rope_rows, rope_rows), rope_rows)
        kp[dst, :] = k_rot
        vp[dst, :] = v_in.astype(F32)
        perm = perm_ref[...]
        for sb in range(rope_rows // QBLK):
            blk = j * (rope_rows // QBLK) + sb
            sl = slice(sb * QBLK, (sb + 1) * QBLK)
            for val, buf, stride, off in ((q_rot, qg, seg, 0), (k_rot, kg, pitch, half_w),
                                          (v_in, vg, pitch, half_w)):
                px = jnp.dot(perm, val[sl].astype(BF16), preferred_element_type=F32)
                for res in range(big_dil):
                    at = pl.ds(pl.multiple_of(res * stride + off + blk * per, per), per)
                    buf[at, :] = px[res * per:(res + 1) * per]
        return carry

    lax.fori_loop(0, seq // rope_rows, rope_body, 0)

    lane = lax.broadcasted_iota(jnp.int32, (1, PAIR), 1)
    head0 = lane < HEAD_DIM
    qi = lax.broadcasted_iota(jnp.int32, (QBLK, 1), 0)
    col = lax.broadcasted_iota(jnp.int32, (1, 2 * KBLK), 1)
    kj = col % KBLK
    left = col < KBLK
    band = jnp.abs(kj - half_w - qi) <= half_w
    ones_st = ((lax.broadcasted_iota(jnp.int32, (2 * KBLK, PAIR), 0) < KBLK)
               == (lax.broadcasted_iota(jnp.int32, (2 * KBLK, PAIR), 1) < HEAD_DIM)).astype(BF16)
    stack_rows = lambda x: jnp.concatenate([jnp.where(head0, x, 0.0), jnp.where(head0, 0.0, x)],
                                           axis=0).astype(BF16)

    for branch, (window, dil) in enumerate(sorted(DILATED_PAIRS, key=lambda wd: -wd[1])):
        assert window // (2 * dil) == half_w
        m_len = seq // dil
        blocks_per_res = m_len // QBLK

        def group_body(it, carry, dil=dil, m_len=m_len, blocks_per_res=blocks_per_res,
                       first=(branch == 0)):
            q_rows, qs, ks, vs, in_range = [], [], [], [], []
            for g in range(ATTN_GROUP):
                blk = it * ATTN_GROUP + g
                res = blk // blocks_per_res
                m0 = (blk % blocks_per_res) * QBLK
                q_start = res + dil * m0
                k_start = pad + res + dil * (m0 - half_w)
                if dil == 1:
                    rows = pl.ds(pl.multiple_of(q_start, QBLK), QBLK)
                    k_rows = pl.ds(pl.multiple_of(k_start, half_w), KBLK)
                else:
                    rows = pl.ds(q_start, QBLK, stride=dil)
                    k_rows = pl.ds(k_start, KBLK, stride=dil)
                q_rows.append(rows)
                if dil == big_dil:
                    g_rows = pl.ds(pl.multiple_of(res * pitch + m0, half_w), KBLK)
                    qb = qg[pl.ds(pl.multiple_of(res * seg + m0, QBLK), QBLK), :]
                    kb, vb = kg[g_rows, :], vg[g_rows, :]
                else:
                    qb, kb, vb = qr[rows, :], kp[k_rows, :], vp[k_rows, :]
                qs.append(qb.astype(BF16))
                ks.append(stack_rows(kb))
                vs.append(jnp.concatenate([stack_rows(vb), ones_st], axis=1))
                kpos = m0 - half_w + kj
                in_range.append((kpos >= 0) & (kpos < m_len))
            sc = _bmm(jnp.stack(qs), jnp.stack(ks), _BNT)
            sc = jnp.stack([jnp.where(in_range[g], jnp.where(band, sc[g], NEG_BIG), NEG_BIG)
                            for g in range(ATTN_GROUP)])
            tile_max = lambda x: jnp.maximum(x[:, :, :KBLK // 2], x[:, :, KBLK // 2:])
            mx0 = jnp.max(tile_max(sc[:, :, :KBLK]), axis=-1, keepdims=True)
            mx1 = jnp.max(tile_max(sc[:, :, KBLK:]), axis=-1, keepdims=True)
            p = jnp.exp(sc - jnp.where(left, mx0, mx1)).astype(BF16)
            nd = lax.dot_general(p, jnp.stack(vs), _BNN, preferred_element_type=F32)
            n_b = nd[:, :, :PAIR]
            d_b = nd[:, :, PAIR:]
            m_b = jnp.where(head0, mx0, mx1)
            for g, rows in enumerate(q_rows):
                if first:
                    m_s[rows, :] = m_b[g]
                    den_s[rows, :] = d_b[g]
                    num_s[rows, :] = n_b[g]
                else:
                    m_old = m_s[rows, :]
                    m_new = jnp.maximum(m_old, m_b[g])
                    alpha = jnp.exp(m_old - m_new)
                    beta = jnp.exp(m_b[g] - m_new)
                    m_s[rows, :] = m_new
                    den_s[rows, :] = alpha * den_s[rows, :] + beta * d_b[g]
                    num_s[rows, :] = alpha * num_s[rows, :] + beta * n_b[g]
            return carry

        lax.fori_loop(0, seq // (QBLK * ATTN_GROUP), group_body, 0)

    def out_body(j, carry):
        rows = pl.ds(pl.multiple_of(j * rope_rows, rope_rows), rope_rows)
        o_ref[0, rows, :] = (num_s[rows, :] / den_s[rows, :]).astype(o_ref.dtype)
        return carry

    lax.fori_loop(0, seq // rope_rows, out_body, 0)


def _dilated_attention(z, col0, width):
    bsz, s, _ = z.shape
    n_pairs = width // PAIR
    max_dil = max(d for _, d in DILATED_PAIRS)
    assert s % (QBLK * max_dil) == 0 and s % (QBLK * ATTN_GROUP) == 0 and col0 % PAIR == 0
    pad = (QBLK // 2) * max_dil
    half = HEAD_DIM // 2
    inv = ROPE_THETA ** (-jnp.arange(half, dtype=F32) / half)
    ang = jnp.arange(s, dtype=F32)[:, None] * inv[None, :]
    cos = jnp.tile(jnp.cos(ang), (1, PAIR // half))
    sin = jnp.tile(jnp.concatenate([-jnp.sin(ang), jnp.sin(ang)], axis=1), (1, 2))
    per = QBLK // max_dil
    row = np.arange(QBLK)
    perm = jnp.asarray(row[None, :] == (row[:, None] % per) * max_dil + row[:, None] // per, BF16)
    lane = np.arange(PAIR)
    partner_lane = np.where(lane % HEAD_DIM < half, lane + half, lane - half)
    swap = jnp.asarray(lane[:, None] == partner_lane[None, :], BF16)
    assert z.dtype == BF16
    cb = col0 // PAIR
    col = lambda off: pl.BlockSpec((1, s, PAIR), lambda b, p: (b, 0, cb + off * n_pairs + p))
    rope_rows = 512
    return pl.pallas_call(
        functools.partial(_dilated_kernel, seq=s, pad=pad, rope_rows=rope_rows, big_dil=max_dil),
        grid=(bsz, n_pairs),
        in_specs=[col(0), col(1), col(2), _resident((s, PAIR)), _resident((s, PAIR)),
                  _resident((PAIR, PAIR)), _resident((QBLK, QBLK))],
        out_specs=pl.BlockSpec((1, s, PAIR), lambda b, p: (b, 0, p)),
        out_shape=jax.ShapeDtypeStruct((bsz, s, width), BF16),
        scratch_shapes=[pltpu.VMEM((s, PAIR), F32),
                        pltpu.VMEM((s + 2 * pad, PAIR), F32),
                        pltpu.VMEM((s + 2 * pad, PAIR), F32),
                        pltpu.VMEM((s, PAIR), F32),
                        pltpu.VMEM((s + QBLK * max_dil, PAIR), F32),
                        pltpu.VMEM((s + QBLK * max_dil, PAIR), F32),
                        pltpu.VMEM((s, PAIR), F32),
                        pltpu.VMEM((s, PAIR), F32),
                        pltpu.VMEM((s, PAIR), F32)],
        compiler_params=_cparams(("parallel", "parallel")),
        name="dilated_attention",
    )(z, z, z, cos, sin, swap, perm)


def _seq_dft_kernel(xc_ref, xs_ref, cm_ref, sm_ref, jm_ref, alt_ref, o_ref, xe_s, yo_s,
                    *, seq, t):
    h = seq // 2
    nb = h // t
    scale = seq ** -0.5
    jm = jm_ref[...]
    row0 = lax.broadcasted_iota(jnp.int32, (t, 1), 0) == 0
    for i in range(nb):
        rows = slice(i * t, (i + 1) * t)
        mirror = slice(seq - (i + 1) * t, seq - i * t)
        pc = jnp.dot(jm, xc_ref[0, mirror, :], preferred_element_type=F32)
        ps = jnp.dot(jm, xs_ref[0, mirror, :], preferred_element_type=F32)
        if i > 0:
            first = slice(seq - i * t, seq - i * t + 1)
            pc = jnp.where(row0, xc_ref[0, first, :].astype(F32), pc)
            ps = jnp.where(row0, xs_ref[0, first, :].astype(F32), ps)
        xe_s[rows, :] = (xc_ref[0, rows, :].astype(F32) + pc).astype(BF16)
        yo_s[rows, :] = (xs_ref[0, rows, :].astype(F32) - ps).astype(BF16)
    xe = xe_s[...]
    p = jnp.dot(cm_ref[...], xe, preferred_element_type=F32)
    q = jnp.dot(sm_ref[...], yo_s[...], preferred_element_type=F32)
    alt = alt_ref[...]
    x_h = xc_ref[0, h:h + 1, :].astype(F32) * scale
    k_odd = lax.broadcasted_iota(jnp.int32, (h, 1), 0) % 2 == 1
    alt_x = jnp.where(k_odd, -x_h, x_h)
    o_ref[0, 0:h, :] = (p - q + alt_x).astype(o_ref.dtype)
    pq = (p + q + alt_x).astype(BF16)
    f_h = scale * jnp.dot(alt, xe, preferred_element_type=F32)[0:1] + x_h
    for i in range(nb):
        rev = jnp.dot(jm, pq[h - (i + 1) * t:h - i * t], preferred_element_type=F32)
        first = f_h if i == 0 else pq[h - i * t:h - i * t + 1].astype(F32)
        o_ref[0, h + i * t:h + (i + 1) * t, :] = jnp.where(row0, first, rev).astype(o_ref.dtype)


def _seq_dft(cm, sm, x2, tn=256, t=128):
    bsz, s, two_d = x2.shape
    d = two_d // 2
    h = s // 2
    assert h % 2 == 0 and h % t == 0 and d % tn == 0
    r = np.arange(t)
    jm = jnp.asarray((r[None, :] == t - r[:, None]) & (r[:, None] >= 1), BF16)
    alt = jnp.asarray(np.where(np.arange(8)[:, None] == 0, 1.0 - 2.0 * (np.arange(h) % 2), 0.0), BF16)
    nj = d // tn
    return pl.pallas_call(
        functools.partial(_seq_dft_kernel, seq=s, t=t),
        grid=(bsz, nj),
        in_specs=[pl.BlockSpec((1, s, tn), lambda b, j: (b, 0, j)),
                  pl.BlockSpec((1, s, tn), lambda b, j: (b, 0, nj + j)),
                  _resident((h, h)), _resident((h, h)), _resident((t, t)), _resident((8, h))],
        out_specs=pl.BlockSpec((1, s, tn), lambda b, j: (b, 0, j)),
        out_shape=jax.ShapeDtypeStruct((bsz, s, d), BF16),
        scratch_shapes=[pltpu.VMEM((h, tn), BF16), pltpu.VMEM((h, tn), BF16)],
        compiler_params=_cparams(("parallel", "parallel")),
        name="seq_dft",
    )(x2, x2, cm, sm, jm, alt)


def _dft_tables(s, d):
    gsz = d // N_FOURIER_GROUPS
    cidx = jnp.arange(gsz, dtype=jnp.int32)
    ang_c = (2.0 * np.pi / gsz) * ((cidx[:, None] * cidx[None, :]) % gsz).astype(F32)
    w_ch = jnp.concatenate([jnp.cos(ang_c), jnp.sin(ang_c)], axis=1) * (gsz ** -0.5)
    q = 64
    h = s // 2
    assert h % q == 0
    sidx = jnp.arange(h, dtype=jnp.int32)[None, :]
    ang = lambda kvals: (2.0 * np.pi / s) * ((kvals[:, None] * sidx) % s).astype(F32)
    ang_hi = ang(q * jnp.arange(h // q, dtype=jnp.int32))[:, None, :]
    ang_lo = ang(jnp.arange(q, dtype=jnp.int32))[None, :, :]
    scale = s ** -0.5
    ch, sh, cl, sl = jnp.cos(ang_hi) * scale, jnp.sin(ang_hi) * scale, jnp.cos(ang_lo), jnp.sin(ang_lo)
    cm = (ch * cl - sh * sl).reshape(h, h).astype(BF16)
    sm = (sh * cl + ch * sl).reshape(h, h).astype(BF16)
    return w_ch.astype(BF16), cm, sm


def kernel(x, c, ada_w, ada_b, norm_g, mix_in, mix_conv, rwkv_w0, rwkv_w1, rwkv_w2, rwkv_a0, rwkv_a1, rwkv_a2, rwkv_g1, rwkv_g2, rwkv_k_k, rwkv_k_a, rwkv_r_k, rwkv_lnx_w, rwkv_lnx_b, mix_out, fnet_w, ffn_up, ffn_conv, ffn_down):
    bsz, s, d = x.shape
    depth = ada_w.shape[0]
    width_a = rwkv_k_k.shape[-1]
    width_b = mix_out.shape[1] - width_a
    in_cols_a = 4 * width_a
    ones_bd = jnp.kron(jnp.eye(width_a // HEAD_DIM, dtype=F32),
                       jnp.ones((HEAD_DIM, HEAD_DIM), F32)).astype(BF16)

    mod = _adaln_mod(c, ada_w, ada_b)
    for l in range(depth):
        sh1, sc1, gt1, sh2, sc2, gt2 = [mod[l, :, j * d:(j + 1) * d] for j in range(6)]
        if l % 2 == 0:
            e = l // 2
            zb, r, v, kk, g, bonus, kd, bb, lw = _mix_prep(
                x, norm_g[l, 0], sc1, sh1, mix_in[e], mix_conv[e], rwkv_w0[e], rwkv_w1[e],
                rwkv_w2[e], rwkv_a0[e], rwkv_a1[e], rwkv_a2[e], rwkv_g1[e], rwkv_g2[e],
                rwkv_k_k[e], rwkv_k_a[e], rwkv_r_k[e], ones_bd)
            y0, y1 = _wkv_scan(r, v, kk, kd, bb, lw)
            yb = _dilated_attention(zb, 0, width_b)
            x = _mix_out(y0, y1, g, bonus, rwkv_lnx_w[e], rwkv_lnx_b[e], ones_bd, yb,
                         mix_out[e], norm_g[l, 1], gt1, x)
        else:
            w_ch, cm, sm = _dft_tables(s, d)
            x2 = _chan_dft(x, norm_g[l, 0], sc1, sh1, w_ch)
            f = _seq_dft(cm, sm, x2)
            x = _out_proj([f], [fnet_w[l // 2]], norm_g[l, 1], gt1, x)
        x = _ffn(x, norm_g[l, 2], sc2, sh2, ffn_up[l], ffn_conv[l], ffn_down[l],
                 norm_g[l, 3], gt2)
    return x
```

```python
import functools

import jax
import jax.numpy as jnp
import numpy as np
from jax import lax
from jax.experimental import pallas as pl
from jax.experimental.pallas import tpu as pltpu

F32 = jnp.float32
BF16 = jnp.bfloat16

HEAD_DIM = 64
DILATED_PAIRS = ((128, 1), (512, 4), (2048, 16))
ROPE_THETA = 10000.0
N_FOURIER_GROUPS = 8
RMS_EPS = 1e-6
LNX_EPS = 64e-5
L2_EPS = 1e-12
NEG_BIG = -1e30

LANES = 128
SUBLANES = 8
VMEM_LIMIT_BYTES = 56 * 1024 * 1024

PAIR = 2 * HEAD_DIM
assert PAIR == LANES
HALO = SUBLANES
CHUNK = 64
QBLK = 128
KBLK = 2 * QBLK
ATTN_GROUP = 4


def _cparams(sem):
    return pltpu.CompilerParams(dimension_semantics=sem, vmem_limit_bytes=VMEM_LIMIT_BYTES)


def _resident(shape):
    zeros = (0,) * len(shape)
    return pl.BlockSpec(shape, lambda *_: zeros, pipeline_mode=pl.Buffered(1))


def _mm(a, b):
    return jnp.dot(a.astype(BF16), b.astype(BF16), preferred_element_type=F32)


def _mm_nt(a, b):
    return lax.dot_general(a.astype(BF16), b.astype(BF16), (((1,), (1,)), ((), ())),
                           preferred_element_type=F32)


def _split(x, n):
    parts = []
    rem = x
    for _ in range(n):
        p = rem.astype(BF16)
        parts.append(p)
        rem = rem - p.astype(F32)
    return parts


def _rmsnorm(x, g):
    return x * lax.rsqrt(jnp.mean(x * x, -1, keepdims=True) + RMS_EPS) * g


def _conv3_rows(z, w, n_rows):
    total = z.shape[0]
    y = pltpu.roll(z, 1, axis=0) * w[0:1] + z * w[1:2] + pltpu.roll(z, total - 1, axis=0) * w[2:3]
    return y[HALO:HALO + n_rows]


def _halo_specs(tm, width, seq, col_block=0):
    per = tm // HALO
    last = seq // HALO - 1

    def prev_map(b, i):
        return (b, jnp.maximum(i * per - 1, 0), col_block)

    def next_map(b, i):
        return (b, jnp.minimum((i + 1) * per, last), col_block)

    return [pl.BlockSpec((1, HALO, width), prev_map),
            pl.BlockSpec((1, tm, width), lambda b, i: (b, i, col_block)),
            pl.BlockSpec((1, HALO, width), next_map)]


def _mod_kernel(c_ref, w_ref, b_ref, o_ref):
    c = c_ref[...]
    cs = c * jax.nn.sigmoid(c)
    o_ref[0] = jnp.dot(cs, w_ref[0], precision=lax.Precision.HIGHEST,
                       preferred_element_type=F32) + b_ref[0]


def _adaln_mod(c, ada_w, ada_b):
    depth, d, n = ada_w.shape
    b = c.shape[0]
    tn = n // 4
    return pl.pallas_call(
        _mod_kernel,
        grid=(depth, n // tn),
        in_specs=[pl.BlockSpec((b, d), lambda l, j: (0, 0)),
                  pl.BlockSpec((1, d, tn), lambda l, j: (l, 0, j)),
                  pl.BlockSpec((1, 1, tn), lambda l, j: (l, 0, j))],
        out_specs=pl.BlockSpec((1, b, tn), lambda l, j: (l, 0, j)),
        out_shape=jax.ShapeDtypeStruct((depth, b, n), F32),
        compiler_params=_cparams(("parallel", "parallel")),
        name="adaln_mod",
    )(c, ada_w, ada_b.reshape(depth, 1, n))


def _chan_dft_kernel(x_ref, g_ref, sc_ref, sh_ref, w_ref, o_ref):
    h = (_rmsnorm(x_ref[0], g_ref[...]) * (1.0 + sc_ref[0]) + sh_ref[0]).astype(BF16)
    d = h.shape[1]
    gsz = w_ref.shape[0]
    w = w_ref[...]
    for grp in range(d // gsz):
        cols = slice(grp * gsz, (grp + 1) * gsz)
        xy = jnp.dot(h[:, cols], w, preferred_element_type=F32)
        o_ref[0, :, cols] = xy[:, :gsz].astype(o_ref.dtype)
        o_ref[0, :, d + grp * gsz:d + (grp + 1) * gsz] = xy[:, gsz:].astype(o_ref.dtype)


def _chan_dft(x, g, sc, sh, w, tm=1024):
    bsz, s, d = x.shape
    assert w.shape[0] % LANES == 0 and d % w.shape[0] == 0
    vec = pl.BlockSpec((1, 1, d), lambda b, i: (b, 0, 0))
    return pl.pallas_call(
        _chan_dft_kernel,
        grid=(bsz, s // tm),
        in_specs=[pl.BlockSpec((1, tm, d), lambda b, i: (b, i, 0)),
                  pl.BlockSpec((1, d), lambda b, i: (0, 0)), vec, vec,
                  _resident(w.shape)],
        out_specs=pl.BlockSpec((1, tm, 2 * d), lambda b, i: (b, i, 0)),
        out_shape=jax.ShapeDtypeStruct((bsz, s, 2 * d), BF16),
        compiler_params=_cparams(("parallel", "parallel")),
        name="chan_dft",
    )(x, g.reshape(1, d), sc.reshape(bsz, 1, d), sh.reshape(bsz, 1, d), w)


def _out_proj_kernel(*refs, n_in):
    y_refs = refs[:n_in]
    w_refs = refs[n_in:2 * n_in]
    g_ref, gt_ref, x_ref, o_ref = refs[2 * n_in:]
    acc = _mm(y_refs[0][0], w_refs[0][...])
    for y_ref, w_ref in zip(y_refs[1:], w_refs[1:]):
        acc = acc + _mm(y_ref[0], w_ref[...])
    o_ref[0] = x_ref[0] + gt_ref[0] * _rmsnorm(acc, g_ref[...])


def _out_proj(ys, ws, g, gt, x, tm=1024):
    bsz, s, d = x.shape
    n_in = len(ys)
    row = lambda width: pl.BlockSpec((1, tm, width), lambda b, i: (b, i, 0))
    return pl.pallas_call(
        functools.partial(_out_proj_kernel, n_in=n_in),
        grid=(bsz, s // tm),
        in_specs=([row(y.shape[-1]) for y in ys] + [_resident(w.shape) for w in ws]
                  + [pl.BlockSpec((1, d), lambda b, i: (0, 0)),
                     pl.BlockSpec((1, 1, d), lambda b, i: (b, 0, 0)), row(d)]),
        out_specs=row(d),
        out_shape=jax.ShapeDtypeStruct((bsz, s, d), F32),
        compiler_params=_cparams(("parallel", "parallel")),
        name="out_proj",
    )(*ys, *[w.astype(BF16) for w in ws], g.reshape(1, d), gt.reshape(bsz, 1, d), x)


def _ffn_kernel(xp_ref, x_ref, xn_ref, g0_ref, sc_ref, sh_ref, up_ref, cw_ref, dn_ref,
                g1_ref, gt_ref, o_ref, *, tm, d_ff, n_chunks):
    i = pl.program_id(1)
    last = pl.num_programs(1) - 1
    x = x_ref[0]
    xh = jnp.concatenate([xp_ref[0], x, xn_ref[0]], axis=0)
    h = _rmsnorm(xh, g0_ref[...]) * (1.0 + sc_ref[0]) + sh_ref[0]
    rows = lax.broadcasted_iota(jnp.int32, (tm + 2 * HALO, 1), 0)
    lo = jnp.where(i > 0, 0, HALO)
    hi = jnp.where(i < last, tm + 2 * HALO, tm + HALO)
    h = jnp.where((rows >= lo) & (rows < hi), h, 0.0).astype(BF16)
    fc = d_ff // n_chunks
    acc = jnp.zeros((tm, x.shape[1]), F32)
    for f in range(n_chunks):
        gsl = slice(f * fc, (f + 1) * fc)
        vsl = slice(d_ff + f * fc, d_ff + (f + 1) * fc)
        zg = _conv3_rows(_mm(h, up_ref[:, gsl]), cw_ref[:, gsl], tm)
        zv = _conv3_rows(_mm(h, up_ref[:, vsl]), cw_ref[:, vsl], tm)
        act = jax.nn.gelu(zg, approximate=True) * zv
        acc = acc + _mm(act, dn_ref[gsl, :])
    o_ref[0] = x + gt_ref[0] * _rmsnorm(acc, g1_ref[...])


def _ffn(x, g0, sc, sh, up, conv, down, g1, gt, tm=512, n_chunks=2):
    bsz, s, d = x.shape
    d_ff = down.shape[0]
    vec = pl.BlockSpec((1, 1, d), lambda b, i: (b, 0, 0))
    gvec = pl.BlockSpec((1, d), lambda b, i: (0, 0))
    return pl.pallas_call(
        functools.partial(_ffn_kernel, tm=tm, d_ff=d_ff, n_chunks=n_chunks),
        grid=(bsz, s // tm),
        in_specs=(_halo_specs(tm, d, s)
                  + [gvec, vec, vec, _resident(up.shape), _resident(conv.shape),
                     _resident(down.shape), gvec, vec]),
        out_specs=pl.BlockSpec((1, tm, d), lambda b, i: (b, i, 0)),
        out_shape=jax.ShapeDtypeStruct((bsz, s, d), F32),
        compiler_params=_cparams(("parallel", "parallel")),
        name="conv_glu_ffn",
    )(x, x, x, g0.reshape(1, d), sc.reshape(bsz, 1, d), sh.reshape(bsz, 1, d),
      up.astype(BF16), conv, down.astype(BF16), g1.reshape(1, d), gt.reshape(bsz, 1, d))


def _head_sum(x, ones_bd, pieces=2):
    return sum(jnp.dot(p, ones_bd, preferred_element_type=F32) for p in _split(x, pieces))


def _mix_prep_kernel(xp_ref, x_ref, xn_ref, g0_ref, sc_ref, sh_ref, win_ref, cw_ref, w1_ref,
                     w2_ref, vec_ref, ones_ref, zb_o, r_o, v_o, kk_o, g_o, bonus_o, kd_o, b_o,
                     lw_o, *, tm, sub, width):
    i = pl.program_id(1)
    last = pl.num_programs(1) - 1
    xh = jnp.concatenate([xp_ref[0], x_ref[0], xn_ref[0]], axis=0)
    h = _rmsnorm(xh, g0_ref[...]) * (1.0 + sc_ref[0]) + sh_ref[0]
    rows = lax.broadcasted_iota(jnp.int32, (tm + 2 * HALO, 1), 0)
    lo = jnp.where(i > 0, 0, HALO)
    hi = jnp.where(i < last, tm + 2 * HALO, tm + HALO)
    h = jnp.where((rows >= lo) & (rows < hi), h, 0.0).astype(BF16)
    cols_a = 4 * width
    ones_bd = ones_ref[...]
    vec = vec_ref[...]
    k_k, k_a, r_k = vec[4:5], vec[5:6], vec[6:7]
    for j in range(tm // sub):
        out = slice(j * sub, (j + 1) * sub)
        hs = h[j * sub:(j + 1) * sub + 2 * HALO]
        za = _conv3_rows(_mm(hs, win_ref[:, :cols_a]), cw_ref[...], sub)
        r = za[:, 0:width]
        k = za[:, width:2 * width]
        v = za[:, 2 * width:3 * width]
        u = za[:, 3 * width:4 * width]

        lora = _mm(u, w1_ref[...])
        g_o[0, out] = _mm(jax.nn.sigmoid(lora[:, 4 * LANES:5 * LANES]), w2_ref[4]).astype(g_o.dtype)
        kk = k * k_k
        kk = kk * lax.rsqrt(_head_sum(kk * kk, ones_bd) + L2_EPS)
        r_o[0, out] = r.astype(r_o.dtype)
        v_o[0, out] = v.astype(v_o.dtype)
        kk_o[0, out] = kk.astype(kk_o.dtype)
        kd_sum = jnp.zeros_like(r)
        for d in range(2):
            q = vec[d:d + 1] + _mm(jnp.tanh(lora[:, d * LANES:(d + 1) * LANES]), w2_ref[d])
            log_decay = -np.exp(-0.5) * jax.nn.sigmoid(q)
            a = jax.nn.sigmoid(vec[2 + d:3 + d]
                               + _mm(lora[:, (2 + d) * LANES:(3 + d) * LANES], w2_ref[2 + d]))
            kd = k * (1.0 + (a - 1.0) * k_a)
            kd_o[d, 0, out] = kd.astype(kd_o.dtype)
            b_o[d, 0, out] = (kk * a).astype(b_o.dtype)
            lw_o[d, 0, out] = log_decay
            kd_sum = kd_sum + kd
        bonus = _head_sum(r * r_k * kd_sum, ones_bd, pieces=1)
        bonus_o[0, out] = (bonus * v).astype(bonus_o.dtype)
        zb_o[0, out] = _mm(hs[HALO:HALO + sub], win_ref[:, cols_a:]).astype(zb_o.dtype)


def _mix_prep(x, g0, sc, sh, w_in, conv, w0, w1, w2, a0, a1, a2, g1, g2, k_k, k_a, r_k, ones_bd,
              tm=512, sub=256):
    bsz, s, d = x.shape
    width = k_k.shape[0]
    lora = w1.shape[-1]
    cols_b = w_in.shape[1] - 4 * width
    pad_c = lambda m: jnp.pad(m, ((0, 0), (0, LANES - m.shape[1])))
    pad_r = lambda m: jnp.pad(m, ((0, LANES - m.shape[0]), (0, 0)))
    assert lora <= LANES and g1.shape[1] == LANES
    w1cat = jnp.concatenate([pad_c(w1[0]), pad_c(w1[1]), pad_c(a1[0]), pad_c(a1[1]), g1],
                            axis=1).astype(BF16)
    w2cat = jnp.stack([pad_r(w2[0]), pad_r(w2[1]), pad_r(a2[0]), pad_r(a2[1]), g2]).astype(BF16)
    vec = jnp.stack([w0[0], w0[1], a0[0], a0[1], k_k, k_a, r_k.reshape(-1),
                     jnp.zeros_like(k_k)])
    out1 = jax.ShapeDtypeStruct((bsz, s, width), BF16)
    out2 = jax.ShapeDtypeStruct((2, bsz, s, width), BF16)
    spec1 = pl.BlockSpec((1, tm, width), lambda b, i: (b, i, 0))
    spec2 = pl.BlockSpec((2, 1, tm, width), lambda b, i: (0, b, i, 0))
    mvec = pl.BlockSpec((1, 1, d), lambda b, i: (b, 0, 0))
    return pl.pallas_call(
        functools.partial(_mix_prep_kernel, tm=tm, sub=sub, width=width),
        grid=(bsz, s // tm),
        in_specs=(_halo_specs(tm, d, s)
                  + [pl.BlockSpec((1, d), lambda b, i: (0, 0)), mvec, mvec,
                     _resident(w_in.shape), _resident(conv.shape), _resident(w1cat.shape),
                     _resident(w2cat.shape), _resident(vec.shape), _resident(ones_bd.shape)]),
        out_specs=([pl.BlockSpec((1, tm, cols_b), lambda b, i: (b, i, 0))]
                   + [spec1] * 5 + [spec2] * 3),
        out_shape=([jax.ShapeDtypeStruct((bsz, s, cols_b), BF16)] + [out1] * 5 + [out2] * 2
                   + [jax.ShapeDtypeStruct((2, bsz, s, width), F32)]),
        compiler_params=_cparams(("parallel", "parallel")),
        name="mix_in_rwkv_prep",
    )(x, x, x, g0.reshape(1, d), sc.reshape(bsz, 1, d), sh.reshape(bsz, 1, d),
      w_in.astype(BF16), conv, w1cat, w2cat, vec, ones_bd)


_BNN = (((2,), (1,)), ((0,), (0,)))
_BNT = (((2,), (2,)), ((0,), (0,)))
_BTN = (((1,), (1,)), ((0,), (0,)))


def _bmm(a, b, dims=_BNN):
    return lax.dot_general(a.astype(BF16), b.astype(BF16), dims, preferred_element_type=F32)


def _stack(x, head0):
    return jnp.concatenate([jnp.where(head0, x, 0.0), jnp.where(head0, 0.0, x)], axis=1)


def _wkv_group(r, k, v, kk, b, lw, ht0, tri, m_strict, m_incl):
    _, c, _ = r.shape
    n2 = 2 * c
    lw3 = _split(lw, 3)
    cum = sum(lax.dot_general(tri, p, _BNN, preferred_element_type=F32) for p in lw3)
    tot = jnp.sum(lw, axis=1, keepdims=True)
    g_inv = jnp.exp(-cum)
    g_rat = jnp.exp(tot - cum)
    lane = lax.broadcasted_iota(jnp.int32, (1, 1, PAIR), 2)
    head0 = lane < HEAD_DIM
    left = lane < c
    kt = kk * jnp.exp(cum - lw)
    rt = r * jnp.exp(cum)
    vs = _stack(v, head0)

    a_all = _bmm(jnp.concatenate([kt, rt], axis=1),
                 jnp.concatenate([_stack(k * g_inv, head0), _stack(b * g_inv, head0)], axis=1),
                 _BNT)
    akk = jnp.where(m_strict, a_all[:, :c, :n2], 0.0)
    akb = jnp.where(m_strict, a_all[:, :c, n2:], 0.0)
    ark = jnp.where(m_incl, a_all[:, c:, :n2], 0.0)
    arb = jnp.where(m_incl, a_all[:, c:, n2:], 0.0)

    eye = (lax.broadcasted_iota(jnp.int32, (1, c, n2), 1)
           == lax.broadcasted_iota(jnp.int32, (1, c, n2), 2) % c).astype(F32)
    n = -akb
    t = eye + n
    p = _bmm(n, _stack(n, left))
    for _ in range(int(np.ceil(np.log2(c))) - 2):
        pt = _bmm(jnp.concatenate([p, t], axis=1), _stack(p, left))
        p = pt[:, :c]
        t = t + pt[:, c:]
    t = t + _bmm(t, _stack(p, left))
    t_hi, t_lo = _split(t, 2)
    ia_t = lax.dot_general((eye + akb).astype(BF16),
                           jnp.concatenate([_stack(t_hi, left), _stack(t_lo, left)], axis=2),
                           _BNN, preferred_element_type=F32)
    t = t + _bmm(t, _stack(eye - ia_t[:, :, :n2] - ia_t[:, :, n2:], left))

    hk = _bmm(jnp.concatenate([kt, rt], axis=1), ht0, _BNT)
    av = _bmm(jnp.concatenate([akk, ark], axis=1), vs)
    u = _bmm(t, _stack(hk[:, :c] + av[:, :c], head0))
    y = hk[:, c:] + av[:, c:] - _bmm(arb, _stack(u, head0))
    same_head = ((lax.broadcasted_iota(jnp.int32, (1, PAIR, PAIR), 1) < HEAD_DIM)
                 == (lax.broadcasted_iota(jnp.int32, (1, PAIR, PAIR), 2) < HEAD_DIM))
    upd_t = _bmm(jnp.concatenate([v, u], axis=1),
                 jnp.concatenate([k * g_rat, -(b * g_rat)], axis=1), _BTN)
    ht_new = jnp.exp(tot) * ht0 + jnp.where(same_head, upd_t, 0.0)
    return y, ht_new


def _wkv_kernel(r0_ref, r1_ref, v0_ref, v1_ref, kk0_ref, kk1_ref, k0_ref, k1_ref, b0_ref, b1_ref,
                lw0_ref, lw1_ref, tri_ref, ms_ref, mi_ref, y0_ref, y1_ref, h_ref, *, n_pairs, nb):
    @pl.when(pl.program_id(1) == 0)
    def _():
        h_ref[...] = jnp.zeros_like(h_ref)

    cells = [(bi, slice(p * PAIR, (p + 1) * PAIR)) for bi in range(nb) for p in range(n_pairs)]

    def group(ref0, ref1):
        return jnp.stack([ref0[bi, :, sl] for bi, sl in cells]
                         + [ref1[bi, :, sl] for bi, sl in cells]).astype(F32)

    per_dir = lambda ref: jnp.stack([ref[0]] * len(cells) + [ref[1]] * len(cells))
    y, h_new = _wkv_group(group(r0_ref, r1_ref), group(k0_ref, k1_ref), group(v0_ref, v1_ref),
                          group(kk0_ref, kk1_ref), group(b0_ref, b1_ref), group(lw0_ref, lw1_ref),
                          h_ref[...], per_dir(tri_ref), per_dir(ms_ref) > 0.5, per_dir(mi_ref) > 0.5)
    h_ref[...] = h_new
    for i, (bi, sl) in enumerate(cells):
        y0_ref[bi, :, sl] = y[i]
        y1_ref[bi, :, sl] = y[len(cells) + i]


def _wkv_scan(r, v, kk, kd, bb, lw):
    bsz, s, width = r.shape
    c = CHUNK
    nc = s // c
    n_pairs = width // PAIR
    t_idx = np.arange(c)
    before = np.stack([t_idx[None, :] < t_idx[:, None], t_idx[None, :] > t_idx[:, None]])
    eye = np.eye(c, dtype=bool)[None]
    tri = jnp.asarray(before | eye, BF16)
    m_strict = jnp.asarray(np.tile(before, (1, 1, 2)), F32)
    m_incl = jnp.asarray(np.tile(before | eye, (1, 1, 2)), F32)

    nb = max(n for n in (4, 2, 1) if bsz % n == 0)
    fwd = pl.BlockSpec((nb, c, width), lambda b, ci: (b, ci, 0))
    bwd = pl.BlockSpec((nb, c, width), lambda b, ci: (b, nc - 1 - ci, 0))
    fwd_d = pl.BlockSpec((None, nb, c, width), lambda b, ci: (0, b, ci, 0))
    bwd_d = pl.BlockSpec((None, nb, c, width), lambda b, ci: (1, b, nc - 1 - ci, 0))
    return pl.pallas_call(
        functools.partial(_wkv_kernel, n_pairs=n_pairs, nb=nb),
        grid=(bsz // nb, nc),
        in_specs=[fwd, bwd] * 3 + [fwd_d, bwd_d] * 3 + [
            _resident(tri.shape), _resident(m_strict.shape), _resident(m_incl.shape)],
        out_specs=[fwd, bwd],
        out_shape=[jax.ShapeDtypeStruct((bsz, s, width), F32)] * 2,
        scratch_shapes=[pltpu.VMEM((2 * nb * n_pairs, PAIR, PAIR), F32)],
        compiler_params=_cparams(("parallel", "arbitrary")),
        name="wkv_scan",
    )(r, r, v, v, kk, kk, kd, kd, bb, bb, lw, lw, tri, m_strict, m_incl)


def _mix_out_kernel(y0_ref, y1_ref, gate_ref, bonus_ref, lw_ref, lb_ref, ones_ref, yb_ref,
                    wa_ref, wb_ref, g_ref, gt_ref, x_ref, o_ref):
    ones_bd = ones_ref[...]
    y = y0_ref[0] + y1_ref[0]
    inv_n = 1.0 / HEAD_DIM
    yc = y - _head_sum(y, ones_bd) * inv_n
    var = _head_sum(yc * yc, ones_bd) * inv_n
    yn = yc * lax.rsqrt(var + LNX_EPS) * lw_ref[...] + lb_ref[...]
    ya = (yn + bonus_ref[0].astype(F32)) * gate_ref[0].astype(F32)
    acc = _mm(ya, wa_ref[...]) + _mm(yb_ref[0], wb_ref[...])
    o_ref[0] = x_ref[0] + gt_ref[0] * _rmsnorm(acc, g_ref[...])


def _mix_out(y0, y1, gate, bonus, lnx_w, lnx_b, ones_bd, yb, w_out, g, gt, x, tm=1024):
    bsz, s, d = x.shape
    width = y0.shape[-1]
    row = lambda w: pl.BlockSpec((1, tm, w), lambda b, i: (b, i, 0))
    vecs = lambda w: pl.BlockSpec((1, w), lambda b, i: (0, 0))
    wa, wb = w_out[:width].astype(BF16), w_out[width:].astype(BF16)
    return pl.pallas_call(
        _mix_out_kernel,
        grid=(bsz, s // tm),
        in_specs=[row(width)] * 4 + [vecs(width), vecs(width), _resident(ones_bd.shape),
                                     row(yb.shape[-1]), _resident(wa.shape), _resident(wb.shape),
                                     vecs(d), pl.BlockSpec((1, 1, d), lambda b, i: (b, 0, 0)), row(d)],
        out_specs=row(d),
        out_shape=jax.ShapeDtypeStruct((bsz, s, d), F32),
        compiler_params=_cparams(("parallel", "parallel")),
        name="rwkv_post_mix_out",
    )(y0, y1, gate, bonus, lnx_w.reshape(1, width), lnx_b.reshape(1, width), ones_bd, yb,
      wa, wb, g.reshape(1, d), gt.reshape(bsz, 1, d), x)


def _rope_rows(x, cos, sin_signed, swap_halves):
    partner = jnp.dot(x, swap_halves, preferred_element_type=F32)
    return x.astype(F32) * cos + partner * sin_signed


def _dilated_kernel(q_ref, k_ref, v_ref, cos_ref, sin_ref, swap_ref, perm_ref, o_ref,
                    qr, kp, vp, qg, kg, vg, m_s, den_s, num_s, *, seq, pad, rope_rows, big_dil):
    scale = HEAD_DIM ** -0.5
    half_w = QBLK // 2
    zeros_pad = jnp.zeros((pad, PAIR), F32)
    kp[0:pad] = zeros_pad
    vp[0:pad] = zeros_pad
    kp[pad + seq:pad + seq + pad] = zeros_pad
    vp[pad + seq:pad + seq + pad] = zeros_pad
    seg = seq // big_dil
    pitch = seg + 2 * half_w
    per = QBLK // big_dil
    zeros_hw = jnp.zeros((half_w, PAIR), F32)
    for res in range(big_dil):
        for buf in (kg, vg):
            buf[res * pitch:res * pitch + half_w] = zeros_hw
            buf[res * pitch + half_w + seg:(res + 1) * pitch] = zeros_hw

    def rope_body(j, carry):
        rows = pl.ds(pl.multiple_of(j * rope_rows, rope_rows), rope_rows)
        cos = cos_ref[rows, :]
        sin = sin_ref[rows, :]
        swap = swap_ref[...]
        q_rot = _rope_rows(q_ref[0, rows, :], cos, sin, swap) * scale
        k_rot = _rope_rows(k_ref[0, rows, :], cos, sin, swap)
        v_in = v_ref[0, rows, :]
        qr[rows, :] = q_rot
        dst = pl.ds(pl.multiple_of(pad + j * rope_rows, rope_rows), rope_rows)
        kp[dst, :] = k_rot
        vp[dst, :] = v_in.astype(F32)
        perm = perm_ref[...]
        for sb in range(rope_rows // QBLK):
            blk = j * (rope_rows // QBLK) + sb
            sl = slice(sb * QBLK, (sb + 1) * QBLK)
            for val, buf, stride, off in ((q_rot, qg, seg, 0), (k_rot, kg, pitch, half_w),
                                          (v_in, vg, pitch, half_w)):
                px = jnp.dot(perm, val[sl].astype(BF16), preferred_element_type=F32)
                for res in range(big_dil):
                    at = pl.ds(pl.multiple_of(res * stride + off + blk * per, per), per)
                    buf[at, :] = px[res * per:(res + 1) * per]
        return carry

    lax.fori_loop(0, seq // rope_rows, rope_body, 0)

    lane = lax.broadcasted_iota(jnp.int32, (1, PAIR), 1)
    head0 = lane < HEAD_DIM
    qi = lax.broadcasted_iota(jnp.int32, (QBLK, 1), 0)
    col = lax.broadcasted_iota(jnp.int32, (1, 2 * KBLK), 1)
    kj = col % KBLK
    left = col < KBLK
    band = jnp.abs(kj - half_w - qi) <= half_w
    ones_st = ((lax.broadcasted_iota(jnp.int32, (2 * KBLK, PAIR), 0) < KBLK)
               == (lax.broadcasted_iota(jnp.int32, (2 * KBLK, PAIR), 1) < HEAD_DIM)).astype(BF16)
    stack_rows = lambda x: jnp.concatenate([jnp.where(head0, x, 0.0), jnp.where(head0, 0.0, x)],
                                           axis=0).astype(BF16)

    for branch, (window, dil) in enumerate(sorted(DILATED_PAIRS, key=lambda wd: -wd[1])):
        assert window // (2 * dil) == half_w
        m_len = seq // dil
        blocks_per_res = m_len // QBLK

        def group_body(it, carry, dil=dil, m_len=m_len, blocks_per_res=blocks_per_res,
                       first=(branch == 0)):
            q_rows, qs, ks, vs, in_range = [], [], [], [], []
            for g in range(ATTN_GROUP):
                blk = it * ATTN_GROUP + g
                res = blk // blocks_per_res
                m0 = (blk % blocks_per_res) * QBLK
                q_start = res + dil * m0
                k_start = pad + res + dil * (m0 - half_w)
                if dil == 1:
                    rows = pl.ds(pl.multiple_of(q_start, QBLK), QBLK)
                    k_rows = pl.ds(pl.multiple_of(k_start, half_w), KBLK)
                else:
                    rows = pl.ds(q_start, QBLK, stride=dil)
                    k_rows = pl.ds(k_start, KBLK, stride=dil)
                q_rows.append(rows)
                if dil == big_dil:
                    g_rows = pl.ds(pl.multiple_of(res * pitch + m0, half_w), KBLK)
                    qb = qg[pl.ds(pl.multiple_of(res * seg + m0, QBLK), QBLK), :]
                    kb, vb = kg[g_rows, :], vg[g_rows, :]
                else:
                    qb, kb, vb = qr[rows, :], kp[k_rows, :], vp[k_rows, :]
                qs.append(qb.astype(BF16))
                ks.append(stack_rows(kb))
                vs.append(jnp.concatenate([stack_rows(vb), ones_st], axis=1))
                kpos = m0 - half_w + kj
                in_range.append((kpos >= 0) & (kpos < m_len))
            sc = _bmm(jnp.stack(qs), jnp.stack(ks), _BNT)
            sc = jnp.stack([jnp.where(in_range[g], jnp.where(band, sc[g], NEG_BIG), NEG_BIG)
                            for g in range(ATTN_GROUP)])
            tile_max = lambda x: jnp.maximum(x[:, :, :KBLK // 2], x[:, :, KBLK // 2:])
            mx0 = jnp.max(tile_max(sc[:, :, :KBLK]), axis=-1, keepdims=True)
            mx1 = jnp.max(tile_max(sc[:, :, KBLK:]), axis=-1, keepdims=True)
            p = jnp.exp(sc - jnp.where(left, mx0, mx1)).astype(BF16)
            nd = lax.dot_general(p, jnp.stack(vs), _BNN, preferred_element_type=F32)
            n_b = nd[:, :, :PAIR]
            d_b = nd[:, :, PAIR:]
            m_b = jnp.where(head0, mx0, mx1)
            for g, rows in enumerate(q_rows):
                if first:
                    m_s[rows, :] = m_b[g]
                    den_s[rows, :] = d_b[g]
                    num_s[rows, :] = n_b[g]
                else:
                    m_old = m_s[rows, :]
                    m_new = jnp.maximum(m_old, m_b[g])
                    alpha = jnp.exp(m_old - m_new)
                    beta = jnp.exp(m_b[g] - m_new)
                    m_s[rows, :] = m_new
                    den_s[rows, :] = alpha * den_s[rows, :] + beta * d_b[g]
                    num_s[rows, :] = alpha * num_s[rows, :] + beta * n_b[g]
            return carry

        lax.fori_loop(0, seq // (QBLK * ATTN_GROUP), group_body, 0)

    def out_body(j, carry):
        rows = pl.ds(pl.multiple_of(j * rope_rows, rope_rows), rope_rows)
        o_ref[0, rows, :] = (num_s[rows, :] / den_s[rows, :]).astype(o_ref.dtype)
        return carry

    lax.fori_loop(0, seq // rope_rows, out_body, 0)


def _dilated_attention(z, col0, width):
    bsz, s, _ = z.shape
    n_pairs = width // PAIR
    max_dil = max(d for _, d in DILATED_PAIRS)
    assert s % (QBLK * max_dil) == 0 and s % (QBLK * ATTN_GROUP) == 0 and col0 % PAIR == 0
    pad = (QBLK // 2) * max_dil
    half = HEAD_DIM // 2
    inv = ROPE_THETA ** (-jnp.arange(half, dtype=F32) / half)
    ang = jnp.arange(s, dtype=F32)[:, None] * inv[None, :]
    cos = jnp.tile(jnp.cos(ang), (1, PAIR // half))
    sin = jnp.tile(jnp.concatenate([-jnp.sin(ang), jnp.sin(ang)], axis=1), (1, 2))
    per = QBLK // max_dil
    row = np.arange(QBLK)
    perm = jnp.asarray(row[None, :] == (row[:, None] % per) * max_dil + row[:, None] // per, BF16)
    lane = np.arange(PAIR)
    partner_lane = np.where(lane % HEAD_DIM < half, lane + half, lane - half)
    swap = jnp.asarray(lane[:, None] == partner_lane[None, :], BF16)
    assert z.dtype == BF16
    cb = col0 // PAIR
    col = lambda off: pl.BlockSpec((1, s, PAIR), lambda b, p: (b, 0, cb + off * n_pairs + p))
    rope_rows = 512
    return pl.pallas_call(
        functools.partial(_dilated_kernel, seq=s, pad=pad, rope_rows=rope_rows, big_dil=max_dil),
        grid=(bsz, n_pairs),
        in_specs=[col(0), col(1), col(2), _resident((s, PAIR)), _resident((s, PAIR)),
                  _resident((PAIR, PAIR)), _resident((QBLK, QBLK))],
        out_specs=pl.BlockSpec((1, s, PAIR), lambda b, p: (b, 0, p)),
        out_shape=jax.ShapeDtypeStruct((bsz, s, width), BF16),
        scratch_shapes=[pltpu.VMEM((s, PAIR), F32),
                        pltpu.VMEM((s + 2 * pad, PAIR), F32),
                        pltpu.VMEM((s + 2 * pad, PAIR), F32),
                        pltpu.VMEM((s, PAIR), F32),
                        pltpu.VMEM((s + QBLK * max_dil, PAIR), F32),
                        pltpu.VMEM((s + QBLK * max_dil, PAIR), F32),
                        pltpu.VMEM((s, PAIR), F32),
                        pltpu.VMEM((s, PAIR), F32),
                        pltpu.VMEM((s, PAIR), F32)],
        compiler_params=_cparams(("parallel", "parallel")),
        name="dilated_attention",
    )(z, z, z, cos, sin, swap, perm)


def _seq_dft_kernel(xc_ref, xs_ref, cm_ref, sm_ref, jm_ref, alt_ref, o_ref, xe_s, yo_s,
                    *, seq, t):
    h = seq // 2
    nb = h // t
    scale = seq ** -0.5
    jm = jm_ref[...]
    row0 = lax.broadcasted_iota(jnp.int32, (t, 1), 0) == 0
    for i in range(nb):
        rows = slice(i * t, (i + 1) * t)
        mirror = slice(seq - (i + 1) * t, seq - i * t)
        pc = jnp.dot(jm, xc_ref[0, mirror, :], preferred_element_type=F32)
        ps = jnp.dot(jm, xs_ref[0, mirror, :], preferred_element_type=F32)
        if i > 0:
            first = slice(seq - i * t, seq - i * t + 1)
            pc = jnp.where(row0, xc_ref[0, first, :].astype(F32), pc)
            ps = jnp.where(row0, xs_ref[0, first, :].astype(F32), ps)
        xe_s[rows, :] = (xc_ref[0, rows, :].astype(F32) + pc).astype(BF16)
        yo_s[rows, :] = (xs_ref[0, rows, :].astype(F32) - ps).astype(BF16)
    xe = xe_s[...]
    p = jnp.dot(cm_ref[...], xe, preferred_element_type=F32)
    q = jnp.dot(sm_ref[...], yo_s[...], preferred_element_type=F32)
    alt = alt_ref[...]
    x_h = xc_ref[0, h:h + 1, :].astype(F32) * scale
    k_odd = lax.broadcasted_iota(jnp.int32, (h, 1), 0) % 2 == 1
    alt_x = jnp.where(k_odd, -x_h, x_h)
    o_ref[0, 0:h, :] = (p - q + alt_x).astype(o_ref.dtype)
    pq = (p + q + alt_x).astype(BF16)
    f_h = scale * jnp.dot(alt, xe, preferred_element_type=F32)[0:1] + x_h
    for i in range(nb):
        rev = jnp.dot(jm, pq[h - (i + 1) * t:h - i * t], preferred_element_type=F32)
        first = f_h if i == 0 else pq[h - i * t:h - i * t + 1].astype(F32)
        o_ref[0, h + i * t:h + (i + 1) * t, :] = jnp.where(row0, first, rev).astype(o_ref.dtype)


def _seq_dft(cm, sm, x2, tn=256, t=128):
    bsz, s, two_d = x2.shape
    d = two_d // 2
    h = s // 2
    assert h % 2 == 0 and h % t == 0 and d % tn == 0
    r = np.arange(t)
    jm = jnp.asarray((r[None, :] == t - r[:, None]) & (r[:, None] >= 1), BF16)
    alt = jnp.asarray(np.where(np.arange(8)[:, None] == 0, 1.0 - 2.0 * (np.arange(h) % 2), 0.0), BF16)
    nj = d // tn
    return pl.pallas_call(
        functools.partial(_seq_dft_kernel, seq=s, t=t),
        grid=(bsz, nj),
        in_specs=[pl.BlockSpec((1, s, tn), lambda b, j: (b, 0, j)),
                  pl.BlockSpec((1, s, tn), lambda b, j: (b, 0, nj + j)),
                  _resident((h, h)), _resident((h, h)), _resident((t, t)), _resident((8, h))],
        out_specs=pl.BlockSpec((1, s, tn), lambda b, j: (b, 0, j)),
        out_shape=jax.ShapeDtypeStruct((bsz, s, d), BF16),
        scratch_shapes=[pltpu.VMEM((h, tn), BF16), pltpu.VMEM((h, tn), BF16)],
        compiler_params=_cparams(("parallel", "parallel")),
        name="seq_dft",
    )(x2, x2, cm, sm, jm, alt)


def _dft_tables(s, d):
    gsz = d // N_FOURIER_GROUPS
    cidx = jnp.arange(gsz, dtype=jnp.int32)
    ang_c = (2.0 * np.pi / gsz) * ((cidx[:, None] * cidx[None, :]) % gsz).astype(F32)
    w_ch = jnp.concatenate([jnp.cos(ang_c), jnp.sin(ang_c)], axis=1) * (gsz ** -0.5)
    q = 64
    h = s // 2
    assert h % q == 0
    sidx = jnp.arange(h, dtype=jnp.int32)[None, :]
    ang = lambda kvals: (2.0 * np.pi / s) * ((kvals[:, None] * sidx) % s).astype(F32)
    ang_hi = ang(q * jnp.arange(h // q, dtype=jnp.int32))[:, None, :]
    ang_lo = ang(jnp.arange(q, dtype=jnp.int32))[None, :, :]
    scale = s ** -0.5
    ch, sh, cl, sl = jnp.cos(ang_hi) * scale, jnp.sin(ang_hi) * scale, jnp.cos(ang_lo), jnp.sin(ang_lo)
    cm = (ch * cl - sh * sl).reshape(h, h).astype(BF16)
    sm = (sh * cl + ch * sl).reshape(h, h).astype(BF16)
    return w_ch.astype(BF16), cm, sm


def kernel(x, c, ada_w, ada_b, norm_g, mix_in, mix_conv, rwkv_w0, rwkv_w1, rwkv_w2, rwkv_a0, rwkv_a1, rwkv_a2, rwkv_g1, rwkv_g2, rwkv_k_k, rwkv_k_a, rwkv_r_k, rwkv_lnx_w, rwkv_lnx_b, mix_out, fnet_w, ffn_up, ffn_conv, ffn_down):
    bsz, s, d = x.shape
    depth = ada_w.shape[0]
    width_a = rwkv_k_k.shape[-1]
    width_b = mix_out.shape[1] - width_a
    in_cols_a = 4 * width_a
    ones_bd = jnp.kron(jnp.eye(width_a // HEAD_DIM, dtype=F32),
                       jnp.ones((HEAD_DIM, HEAD_DIM), F32)).astype(BF16)

    mod = _adaln_mod(c, ada_w, ada_b)
    for l in range(depth):
        sh1, sc1, gt1, sh2, sc2, gt2 = [mod[l, :, j * d:(j + 1) * d] for j in range(6)]
        if l % 2 == 0:
            e = l // 2
            zb, r, v, kk, g, bonus, kd, bb, lw = _mix_prep(
                x, norm_g[l, 0], sc1, sh1, mix_in[e], mix_conv[e], rwkv_w0[e], rwkv_w1[e],
                rwkv_w2[e], rwkv_a0[e], rwkv_a1[e], rwkv_a2[e], rwkv_g1[e], rwkv_g2[e],
                rwkv_k_k[e], rwkv_k_a[e], rwkv_r_k[e], ones_bd)
            y0, y1 = _wkv_scan(r, v, kk, kd, bb, lw)
            yb = _dilated_attention(zb, 0, width_b)
            x = _mix_out(y0, y1, g, bonus, rwkv_lnx_w[e], rwkv_lnx_b[e], ones_bd, yb,
                         mix_out[e], norm_g[l, 1], gt1, x)
        else:
            w_ch, cm, sm = _dft_tables(s, d)
            x2 = _chan_dft(x, norm_g[l, 0], sc1, sh1, w_ch)
            f = _seq_dft(cm, sm, x2)
            x = _out_proj([f], [fnet_w[l // 2]], norm_g[l, 1], gt1, x)
        x = _ffn(x, norm_g[l, 2], sc2, sh2, ffn_up[l], ffn_conv[l], ffn_down[l],
                 norm_g[l, 3], gt2)
    return x
```

```python
import functools

import jax
import jax.numpy as jnp
import numpy as np
from jax import lax
from jax.experimental import pallas as pl
from jax.experimental.pallas import tpu as pltpu

F32 = jnp.float32
BF16 = jnp.bfloat16

HEAD_DIM = 64
DILATED_PAIRS = ((128, 1), (512, 4), (2048, 16))
ROPE_THETA = 10000.0
N_FOURIER_GROUPS = 8
RMS_EPS = 1e-6
LNX_EPS = 64e-5
L2_EPS = 1e-12
NEG_BIG = -1e30

LANES = 128
SUBLANES = 8
VMEM_LIMIT_BYTES = 56 * 1024 * 1024

PAIR = 2 * HEAD_DIM
assert PAIR == LANES
HALO = SUBLANES
CHUNK = 64
QBLK = 128
KBLK = 2 * QBLK
ATTN_GROUP = 4


def _cparams(sem):
    return pltpu.CompilerParams(dimension_semantics=sem, vmem_limit_bytes=VMEM_LIMIT_BYTES)


def _resident(shape):
    zeros = (0,) * len(shape)
    return pl.BlockSpec(shape, lambda *_: zeros, pipeline_mode=pl.Buffered(1))


def _mm(a, b):
    return jnp.dot(a.astype(BF16), b.astype(BF16), preferred_element_type=F32)


def _mm_nt(a, b):
    return lax.dot_general(a.astype(BF16), b.astype(BF16), (((1,), (1,)), ((), ())),
                           preferred_element_type=F32)


def _split(x, n):
    parts = []
    rem = x
    for _ in range(n):
        p = rem.astype(BF16)
        parts.append(p)
        rem = rem - p.astype(F32)
    return parts


def _rmsnorm(x, g):
    return x * lax.rsqrt(jnp.mean(x * x, -1, keepdims=True) + RMS_EPS) * g


def _conv3_rows(z, w, n_rows):
    total = z.shape[0]
    y = pltpu.roll(z, 1, axis=0) * w[0:1] + z * w[1:2] + pltpu.roll(z, total - 1, axis=0) * w[2:3]
    return y[HALO:HALO + n_rows]


def _halo_specs(tm, width, seq, col_block=0):
    per = tm // HALO
    last = seq // HALO - 1

    def prev_map(b, i):
        return (b, jnp.maximum(i * per - 1, 0), col_block)

    def next_map(b, i):
        return (b, jnp.minimum((i + 1) * per, last), col_block)

    return [pl.BlockSpec((1, HALO, width), prev_map),
            pl.BlockSpec((1, tm, width), lambda b, i: (b, i, col_block)),
            pl.BlockSpec((1, HALO, width), next_map)]


def _mod_kernel(c_ref, w_ref, b_ref, o_ref):
    c = c_ref[...]
    cs = c * jax.nn.sigmoid(c)
    o_ref[0] = jnp.dot(cs, w_ref[0], precision=lax.Precision.HIGHEST,
                       preferred_element_type=F32) + b_ref[0]


def _adaln_mod(c, ada_w, ada_b):
    depth, d, n = ada_w.shape
    b = c.shape[0]
    tn = n // 4
    return pl.pallas_call(
        _mod_kernel,
        grid=(depth, n // tn),
        in_specs=[pl.BlockSpec((b, d), lambda l, j: (0, 0)),
                  pl.BlockSpec((1, d, tn), lambda l, j: (l, 0, j)),
                  pl.BlockSpec((1, 1, tn), lambda l, j: (l, 0, j))],
        out_specs=pl.BlockSpec((1, b, tn), lambda l, j: (l, 0, j)),
        out_shape=jax.ShapeDtypeStruct((depth, b, n), F32),
        compiler_params=_cparams(("parallel", "parallel")),
        name="adaln_mod",
    )(c, ada_w, ada_b.reshape(depth, 1, n))


def _chan_dft_kernel(x_ref, g_ref, sc_ref, sh_ref, w_ref, o_ref):
    h = (_rmsnorm(x_ref[0], g_ref[...]) * (1.0 + sc_ref[0]) + sh_ref[0]).astype(BF16)
    d = h.shape[1]
    gsz = w_ref.shape[0]
    w = w_ref[...]
    for grp in range(d // gsz):
        cols = slice(grp * gsz, (grp + 1) * gsz)
        xy = jnp.dot(h[:, cols], w, preferred_element_type=F32)
        o_ref[0, :, cols] = xy[:, :gsz].astype(o_ref.dtype)
        o_ref[0, :, d + grp * gsz:d + (grp + 1) * gsz] = xy[:, gsz:].astype(o_ref.dtype)


def _chan_dft(x, g, sc, sh, w, tm=1024):
    bsz, s, d = x.shape
    assert w.shape[0] % LANES == 0 and d % w.shape[0] == 0
    vec = pl.BlockSpec((1, 1, d), lambda b, i: (b, 0, 0))
    return pl.pallas_call(
        _chan_dft_kernel,
        grid=(bsz, s // tm),
        in_specs=[pl.BlockSpec((1, tm, d), lambda b, i: (b, i, 0)),
                  pl.BlockSpec((1, d), lambda b, i: (0, 0)), vec, vec,
                  _resident(w.shape)],
        out_specs=pl.BlockSpec((1, tm, 2 * d), lambda b, i: (b, i, 0)),
        out_shape=jax.ShapeDtypeStruct((bsz, s, 2 * d), BF16),
        compiler_params=_cparams(("parallel", "parallel")),
        name="chan_dft",
    )(x, g.reshape(1, d), sc.reshape(bsz, 1, d), sh.reshape(bsz, 1, d), w)


def _out_proj_kernel(*refs, n_in):
    y_refs = refs[:n_in]
    w_refs = refs[n_in:2 * n_in]
    g_ref, gt_ref, x_ref, o_ref = refs[2 * n_in:]
    acc = _mm(y_refs[0][0], w_refs[0][...])
    for y_ref, w_ref in zip(y_refs[1:], w_refs[1:]):
        acc = acc + _mm(y_ref[0], w_ref[...])
    o_ref[0] = x_ref[0] + gt_ref[0] * _rmsnorm(acc, g_ref[...])


def _out_proj(ys, ws, g, gt, x, tm=1024):
    bsz, s, d = x.shape
    n_in = len(ys)
    row = lambda width: pl.BlockSpec((1, tm, width), lambda b, i: (b, i, 0))
    return pl.pallas_call(
        functools.partial(_out_proj_kernel, n_in=n_in),
        grid=(bsz, s // tm),
        in_specs=([row(y.shape[-1]) for y in ys] + [_resident(w.shape) for w in ws]
                  + [pl.BlockSpec((1, d), lambda b, i: (0, 0)),
                     pl.BlockSpec((1, 1, d), lambda b, i: (b, 0, 0)), row(d)]),
        out_specs=row(d),
        out_shape=jax.ShapeDtypeStruct((bsz, s, d), F32),
        compiler_params=_cparams(("parallel", "parallel")),
        name="out_proj",
    )(*ys, *[w.astype(BF16) for w in ws], g.reshape(1, d), gt.reshape(bsz, 1, d), x)


def _ffn_kernel(xp_ref, x_ref, xn_ref, g0_ref, sc_ref, sh_ref, up_ref, cw_ref, dn_ref,
                g1_ref, gt_ref, o_ref, *, tm, d_ff, n_chunks):
    i = pl.program_id(1)
    last = pl.num_programs(1) - 1
    x = x_ref[0]
    xh = jnp.concatenate([xp_ref[0], x, xn_ref[0]], axis=0)
    h = _rmsnorm(xh, g0_ref[...]) * (1.0 + sc_ref[0]) + sh_ref[0]
    rows = lax.broadcasted_iota(jnp.int32, (tm + 2 * HALO, 1), 0)
    lo = jnp.where(i > 0, 0, HALO)
    hi = jnp.where(i < last, tm + 2 * HALO, tm + HALO)
    h = jnp.where((rows >= lo) & (rows < hi), h, 0.0).astype(BF16)
    fc = d_ff // n_chunks
    acc = jnp.zeros((tm, x.shape[1]), F32)
    for f in range(n_chunks):
        gsl = slice(f * fc, (f + 1) * fc)
        vsl = slice(d_ff + f * fc, d_ff + (f + 1) * fc)
        zg = _conv3_rows(_mm(h, up_ref[:, gsl]), cw_ref[:, gsl], tm)
        zv = _conv3_rows(_mm(h, up_ref[:, vsl]), cw_ref[:, vsl], tm)
        act = jax.nn.gelu(zg, approximate=True) * zv
        acc = acc + _mm(act, dn_ref[gsl, :])
    o_ref[0] = x + gt_ref[0] * _rmsnorm(acc, g1_ref[...])


def _ffn(x, g0, sc, sh, up, conv, down, g1, gt, tm=512, n_chunks=1):
    bsz, s, d = x.shape
    d_ff = down.shape[0]
    vec = pl.BlockSpec((1, 1, d), lambda b, i: (b, 0, 0))
    gvec = pl.BlockSpec((1, d), lambda b, i: (0, 0))
    return pl.pallas_call(
        functools.partial(_ffn_kernel, tm=tm, d_ff=d_ff, n_chunks=n_chunks),
        grid=(bsz, s // tm),
        in_specs=(_halo_specs(tm, d, s)
                  + [gvec, vec, vec, _resident(up.shape), _resident(conv.shape),
                     _resident(down.shape), gvec, vec]),
        out_specs=pl.BlockSpec((1, tm, d), lambda b, i: (b, i, 0)),
        out_shape=jax.ShapeDtypeStruct((bsz, s, d), F32),
        compiler_params=_cparams(("parallel", "parallel")),
        name="conv_glu_ffn",
    )(x, x, x, g0.reshape(1, d), sc.reshape(bsz, 1, d), sh.reshape(bsz, 1, d),
      up.astype(BF16), conv, down.astype(BF16), g1.reshape(1, d), gt.reshape(bsz, 1, d))


def _head_sum(x, ones_bd, pieces=2):
    return sum(jnp.dot(p, ones_bd, preferred_element_type=F32) for p in _split(x, pieces))


def _mix_prep_kernel(xp_ref, x_ref, xn_ref, g0_ref, sc_ref, sh_ref, win_ref, cw_ref, w1_ref,
                     w2_ref, vec_ref, ones_ref, zb_o, r_o, v_o, kk_o, g_o, bonus_o, kd_o, b_o,
                     lw_o, *, tm, sub, width):
    i = pl.program_id(1)
    last = pl.num_programs(1) - 1
    xh = jnp.concatenate([xp_ref[0], x_ref[0], xn_ref[0]], axis=0)
    h = _rmsnorm(xh, g0_ref[...]) * (1.0 + sc_ref[0]) + sh_ref[0]
    rows = lax.broadcasted_iota(jnp.int32, (tm + 2 * HALO, 1), 0)
    lo = jnp.where(i > 0, 0, HALO)
    hi = jnp.where(i < last, tm + 2 * HALO, tm + HALO)
    h = jnp.where((rows >= lo) & (rows < hi), h, 0.0).astype(BF16)
    cols_a = 4 * width
    ones_bd = ones_ref[...]
    vec = vec_ref[...]
    k_k, k_a, r_k = vec[4:5], vec[5:6], vec[6:7]
    for j in range(tm // sub):
        out = slice(j * sub, (j + 1) * sub)
        hs = h[j * sub:(j + 1) * sub + 2 * HALO]
        za = _conv3_rows(_mm(hs, win_ref[:, :cols_a]), cw_ref[...], sub)
        r = za[:, 0:width]
        k = za[:, width:2 * width]
        v = za[:, 2 * width:3 * width]
        u = za[:, 3 * width:4 * width]

        lora = _mm(u, w1_ref[...])
        g_o[0, out] = _mm(jax.nn.sigmoid(lora[:, 4 * LANES:5 * LANES]), w2_ref[4]).astype(g_o.dtype)
        kk = k * k_k
        kk = kk * lax.rsqrt(_head_sum(kk * kk, ones_bd) + L2_EPS)
        r_o[0, out] = r.astype(r_o.dtype)
        v_o[0, out] = v.astype(v_o.dtype)
        kk_o[0, out] = kk.astype(kk_o.dtype)
        kd_sum = jnp.zeros_like(r)
        for d in range(2):
            q = vec[d:d + 1] + _mm(jnp.tanh(lora[:, d * LANES:(d + 1) * LANES]), w2_ref[d])
            log_decay = -np.exp(-0.5) * jax.nn.sigmoid(q)
            a = jax.nn.sigmoid(vec[2 + d:3 + d]
                               + _mm(lora[:, (2 + d) * LANES:(3 + d) * LANES], w2_ref[2 + d]))
            kd = k * (1.0 + (a - 1.0) * k_a)
            kd_o[d, 0, out] = kd.astype(kd_o.dtype)
            b_o[d, 0, out] = (kk * a).astype(b_o.dtype)
            lw_o[d, 0, out] = log_decay
            kd_sum = kd_sum + kd
        bonus = _head_sum(r * r_k * kd_sum, ones_bd, pieces=1)
        bonus_o[0, out] = (bonus * v).astype(bonus_o.dtype)
        zb_o[0, out] = _mm(hs[HALO:HALO + sub], win_ref[:, cols_a:]).astype(zb_o.dtype)


def _mix_prep(x, g0, sc, sh, w_in, conv, w0, w1, w2, a0, a1, a2, g1, g2, k_k, k_a, r_k, ones_bd,
              tm=512, sub=256):
    bsz, s, d = x.shape
    width = k_k.shape[0]
    lora = w1.shape[-1]
    cols_b = w_in.shape[1] - 4 * width
    pad_c = lambda m: jnp.pad(m, ((0, 0), (0, LANES - m.shape[1])))
    pad_r = lambda m: jnp.pad(m, ((0, LANES - m.shape[0]), (0, 0)))
    assert lora <= LANES and g1.shape[1] == LANES
    w1cat = jnp.concatenate([pad_c(w1[0]), pad_c(w1[1]), pad_c(a1[0]), pad_c(a1[1]), g1],
                            axis=1).astype(BF16)
    w2cat = jnp.stack([pad_r(w2[0]), pad_r(w2[1]), pad_r(a2[0]), pad_r(a2[1]), g2]).astype(BF16)
    vec = jnp.stack([w0[0], w0[1], a0[0], a0[1], k_k, k_a, r_k.reshape(-1),
                     jnp.zeros_like(k_k)])
    out1 = jax.ShapeDtypeStruct((bsz, s, width), BF16)
    out2 = jax.ShapeDtypeStruct((2, bsz, s, width), BF16)
    spec1 = pl.BlockSpec((1, tm, width), lambda b, i: (b, i, 0))
    spec2 = pl.BlockSpec((2, 1, tm, width), lambda b, i: (0, b, i, 0))
    mvec = pl.BlockSpec((1, 1, d), lambda b, i: (b, 0, 0))
    return pl.pallas_call(
        functools.partial(_mix_prep_kernel, tm=tm, sub=sub, width=width),
        grid=(bsz, s // tm),
        in_specs=(_halo_specs(tm, d, s)
                  + [pl.BlockSpec((1, d), lambda b, i: (0, 0)), mvec, mvec,
                     _resident(w_in.shape), _resident(conv.shape), _resident(w1cat.shape),
                     _resident(w2cat.shape), _resident(vec.shape), _resident(ones_bd.shape)]),
        out_specs=([pl.BlockSpec((1, tm, cols_b), lambda b, i: (b, i, 0))]
                   + [spec1] * 5 + [spec2] * 3),
        out_shape=([jax.ShapeDtypeStruct((bsz, s, cols_b), BF16)] + [out1] * 5 + [out2] * 2
                   + [jax.ShapeDtypeStruct((2, bsz, s, width), F32)]),
        compiler_params=_cparams(("parallel", "parallel")),
        name="mix_in_rwkv_prep",
    )(x, x, x, g0.reshape(1, d), sc.reshape(bsz, 1, d), sh.reshape(bsz, 1, d),
      w_in.astype(BF16), conv, w1cat, w2cat, vec, ones_bd)


_BNN = (((2,), (1,)), ((0,), (0,)))
_BNT = (((2,), (2,)), ((0,), (0,)))
_BTN = (((1,), (1,)), ((0,), (0,)))


def _bmm(a, b, dims=_BNN):
    return lax.dot_general(a.astype(BF16), b.astype(BF16), dims, preferred_element_type=F32)


def _stack(x, head0):
    return jnp.concatenate([jnp.where(head0, x, 0.0), jnp.where(head0, 0.0, x)], axis=1)


def _wkv_group(r, k, v, kk, b, lw, ht0, tri, m_strict, m_incl):
    _, c, _ = r.shape
    n2 = 2 * c
    lw3 = _split(lw, 3)
    cum = sum(lax.dot_general(tri, p, _BNN, preferred_element_type=F32) for p in lw3)
    tot = jnp.sum(lw, axis=1, keepdims=True)
    g_inv = jnp.exp(-cum)
    g_rat = jnp.exp(tot - cum)
    lane = lax.broadcasted_iota(jnp.int32, (1, 1, PAIR), 2)
    head0 = lane < HEAD_DIM
    left = lane < c
    kt = kk * jnp.exp(cum - lw)
    rt = r * jnp.exp(cum)
    vs = _stack(v, head0)

    a_all = _bmm(jnp.concatenate([kt, rt], axis=1),
                 jnp.concatenate([_stack(k * g_inv, head0), _stack(b * g_inv, head0)], axis=1),
                 _BNT)
    akk = jnp.where(m_strict, a_all[:, :c, :n2], 0.0)
    akb = jnp.where(m_strict, a_all[:, :c, n2:], 0.0)
    ark = jnp.where(m_incl, a_all[:, c:, :n2], 0.0)
    arb = jnp.where(m_incl, a_all[:, c:, n2:], 0.0)

    eye = (lax.broadcasted_iota(jnp.int32, (1, c, n2), 1)
           == lax.broadcasted_iota(jnp.int32, (1, c, n2), 2) % c).astype(F32)
    n = -akb
    t = eye + n
    p = _bmm(n, _stack(n, left))
    for _ in range(int(np.ceil(np.log2(c))) - 2):
        pt = _bmm(jnp.concatenate([p, t], axis=1), _stack(p, left))
        p = pt[:, :c]
        t = t + pt[:, c:]
    t = t + _bmm(t, _stack(p, left))
    t_hi, t_lo = _split(t, 2)
    ia_t = lax.dot_general((eye + akb).astype(BF16),
                           jnp.concatenate([_stack(t_hi, left), _stack(t_lo, left)], axis=2),
                           _BNN, preferred_element_type=F32)
    t = t + _bmm(t, _stack(eye - ia_t[:, :, :n2] - ia_t[:, :, n2:], left))

    hk = _bmm(jnp.concatenate([kt, rt], axis=1), ht0, _BNT)
    av = _bmm(jnp.concatenate([akk, ark], axis=1), vs)
    u = _bmm(t, _stack(hk[:, :c] + av[:, :c], head0))
    y = hk[:, c:] + av[:, c:] - _bmm(arb, _stack(u, head0))
    same_head = ((lax.broadcasted_iota(jnp.int32, (1, PAIR, PAIR), 1) < HEAD_DIM)
                 == (lax.broadcasted_iota(jnp.int32, (1, PAIR, PAIR), 2) < HEAD_DIM))
    upd_t = _bmm(jnp.concatenate([v, u], axis=1),
                 jnp.concatenate([k * g_rat, -(b * g_rat)], axis=1), _BTN)
    ht_new = jnp.exp(tot) * ht0 + jnp.where(same_head, upd_t, 0.0)
    return y, ht_new


def _wkv_kernel(r0_ref, r1_ref, v0_ref, v1_ref, kk0_ref, kk1_ref, k0_ref, k1_ref, b0_ref, b1_ref,
                lw0_ref, lw1_ref, tri_ref, ms_ref, mi_ref, y0_ref, y1_ref, h_ref, *, n_pairs, nb):
    @pl.when(pl.program_id(1) == 0)
    def _():
        h_ref[...] = jnp.zeros_like(h_ref)

    cells = [(bi, slice(p * PAIR, (p + 1) * PAIR)) for bi in range(nb) for p in range(n_pairs)]

    def group(ref0, ref1):
        return jnp.stack([ref0[bi, :, sl] for bi, sl in cells]
                         + [ref1[bi, :, sl] for bi, sl in cells]).astype(F32)

    per_dir = lambda ref: jnp.stack([ref[0]] * len(cells) + [ref[1]] * len(cells))
    y, h_new = _wkv_group(group(r0_ref, r1_ref), group(k0_ref, k1_ref), group(v0_ref, v1_ref),
                          group(kk0_ref, kk1_ref), group(b0_ref, b1_ref), group(lw0_ref, lw1_ref),
                          h_ref[...], per_dir(tri_ref), per_dir(ms_ref) > 0.5, per_dir(mi_ref) > 0.5)
    h_ref[...] = h_new
    for i, (bi, sl) in enumerate(cells):
        y0_ref[bi, :, sl] = y[i]
        y1_ref[bi, :, sl] = y[len(cells) + i]


def _wkv_scan(r, v, kk, kd, bb, lw):
    bsz, s, width = r.shape
    c = CHUNK
    nc = s // c
    n_pairs = width // PAIR
    t_idx = np.arange(c)
    before = np.stack([t_idx[None, :] < t_idx[:, None], t_idx[None, :] > t_idx[:, None]])
    eye = np.eye(c, dtype=bool)[None]
    tri = jnp.asarray(before | eye, BF16)
    m_strict = jnp.asarray(np.tile(before, (1, 1, 2)), F32)
    m_incl = jnp.asarray(np.tile(before | eye, (1, 1, 2)), F32)

    nb = max(n for n in (4, 2, 1) if bsz % n == 0)
    fwd = pl.BlockSpec((nb, c, width), lambda b, ci: (b, ci, 0))
    bwd = pl.BlockSpec((nb, c, width), lambda b, ci: (b, nc - 1 - ci, 0))
    fwd_d = pl.BlockSpec((None, nb, c, width), lambda b, ci: (0, b, ci, 0))
    bwd_d = pl.BlockSpec((None, nb, c, width), lambda b, ci: (1, b, nc - 1 - ci, 0))
    return pl.pallas_call(
        functools.partial(_wkv_kernel, n_pairs=n_pairs, nb=nb),
        grid=(bsz // nb, nc),
        in_specs=[fwd, bwd] * 3 + [fwd_d, bwd_d] * 3 + [
            _resident(tri.shape), _resident(m_strict.shape), _resident(m_incl.shape)],
        out_specs=[fwd, bwd],
        out_shape=[jax.ShapeDtypeStruct((bsz, s, width), F32)] * 2,
        scratch_shapes=[pltpu.VMEM((2 * nb * n_pairs, PAIR, PAIR), F32)],
        compiler_params=_cparams(("parallel", "arbitrary")),
        name="wkv_scan",
    )(r, r, v, v, kk, kk, kd, kd, bb, bb, lw, lw, tri, m_strict, m_incl)


def _mix_out_kernel(y0_ref, y1_ref, gate_ref, bonus_ref, lw_ref, lb_ref, ones_ref, yb_ref,
                    wa_ref, wb_ref, g_ref, gt_ref, x_ref, o_ref):
    ones_bd = ones_ref[...]
    y = y0_ref[0] + y1_ref[0]
    inv_n = 1.0 / HEAD_DIM
    yc = y - _head_sum(y, ones_bd) * inv_n
    var = _head_sum(yc * yc, ones_bd) * inv_n
    yn = yc * lax.rsqrt(var + LNX_EPS) * lw_ref[...] + lb_ref[...]
    ya = (yn + bonus_ref[0].astype(F32)) * gate_ref[0].astype(F32)
    acc = _mm(ya, wa_ref[...]) + _mm(yb_ref[0], wb_ref[...])
    o_ref[0] = x_ref[0] + gt_ref[0] * _rmsnorm(acc, g_ref[...])


def _mix_out(y0, y1, gate, bonus, lnx_w, lnx_b, ones_bd, yb, w_out, g, gt, x, tm=1024):
    bsz, s, d = x.shape
    width = y0.shape[-1]
    row = lambda w: pl.BlockSpec((1, tm, w), lambda b, i: (b, i, 0))
    vecs = lambda w: pl.BlockSpec((1, w), lambda b, i: (0, 0))
    wa, wb = w_out[:width].astype(BF16), w_out[width:].astype(BF16)
    return pl.pallas_call(
        _mix_out_kernel,
        grid=(bsz, s // tm),
        in_specs=[row(width)] * 4 + [vecs(width), vecs(width), _resident(ones_bd.shape),
                                     row(yb.shape[-1]), _resident(wa.shape), _resident(wb.shape),
                                     vecs(d), pl.BlockSpec((1, 1, d), lambda b, i: (b, 0, 0)), row(d)],
        out_specs=row(d),
        out_shape=jax.ShapeDtypeStruct((bsz, s, d), F32),
        compiler_params=_cparams(("parallel", "parallel")),
        name="rwkv_post_mix_out",
    )(y0, y1, gate, bonus, lnx_w.reshape(1, width), lnx_b.reshape(1, width), ones_bd, yb,
      wa, wb, g.reshape(1, d), gt.reshape(bsz, 1, d), x)


def _rope_rows(x, cos, sin_signed, swap_halves):
    partner = jnp.dot(x, swap_halves, preferred_element_type=F32)
    return x.astype(F32) * cos + partner * sin_signed


def _dilated_kernel(q_ref, k_ref, v_ref, cos_ref, sin_ref, swap_ref, perm_ref, o_ref,
                    qr, kp, vp, qg, kg, vg, m_s, den_s, num_s, *, seq, pad, rope_rows, big_dil):
    scale = HEAD_DIM ** -0.5
    half_w = QBLK // 2
    zeros_pad = jnp.zeros((pad, PAIR), F32)
    kp[0:pad] = zeros_pad
    vp[0:pad] = zeros_pad
    kp[pad + seq:pad + seq + pad] = zeros_pad
    vp[pad + seq:pad + seq + pad] = zeros_pad
    seg = seq // big_dil
    pitch = seg + 2 * half_w
    per = QBLK // big_dil
    zeros_hw = jnp.zeros((half_w, PAIR), F32)
    for res in range(big_dil):
        for buf in (kg, vg):
            buf[res * pitch:res * pitch + half_w] = zeros_hw
            buf[res * pitch + half_w + seg:(res + 1) * pitch] = zeros_hw

    def rope_body(j, carry):
        rows = pl.ds(pl.multiple_of(j * rope_rows, rope_rows), rope_rows)
        cos = cos_ref[rows, :]
        sin = sin_ref[rows, :]
        swap = swap_ref[...]
        q_rot = _rope_rows(q_ref[0, rows, :], cos, sin, swap) * scale
        k_rot = _rope_rows(k_ref[0, rows, :], cos, sin, swap)
        v_in = v_ref[0, rows, :]
        qr[rows, :] = q_rot
        dst = pl.ds(pl.multiple_of(pad + j * rope_rows, rope_rows), rope_rows)
        kp[dst, :] = k_rot
        vp[dst, :] = v_in.astype(F32)
        perm = perm_ref[...]
        for sb in range(rope_rows // QBLK):
            blk = j * (rope_rows // QBLK) + sb
            sl = slice(sb * QBLK, (sb + 1) * QBLK)
            for val, buf, stride, off in ((q_rot, qg, seg, 0), (k_rot, kg, pitch, half_w),
                                          (v_in, vg, pitch, half_w)):
                px = jnp.dot(perm, val[sl].astype(BF16), preferred_element_type=F32)
                for res in range(big_dil):
                    at = pl.ds(pl.multiple_of(res * stride + off + blk * per, per), per)
                    buf[at, :] = px[res * per:(res + 1) * per]
        return carry

    lax.fori_loop(0, seq // rope_rows, rope_body, 0)

    lane = lax.broadcasted_iota(jnp.int32, (1, PAIR), 1)
    head0 = lane < HEAD_DIM
    qi = lax.broadcasted_iota(jnp.int32, (QBLK, 1), 0)
    col = lax.broadcasted_iota(jnp.int32, (1, 2 * KBLK), 1)
    kj = col % KBLK
    left = col < KBLK
    band = jnp.abs(kj - half_w - qi) <= half_w
    ones_st = ((lax.broadcasted_iota(jnp.int32, (2 * KBLK, PAIR), 0) < KBLK)
               == (lax.broadcasted_iota(jnp.int32, (2 * KBLK, PAIR), 1) < HEAD_DIM)).astype(BF16)
    stack_rows = lambda x: jnp.concatenate([jnp.where(head0, x, 0.0), jnp.where(head0, 0.0, x)],
                                           axis=0).astype(BF16)

    for branch, (window, dil) in enumerate(sorted(DILATED_PAIRS, key=lambda wd: -wd[1])):
        assert window // (2 * dil) == half_w
        m_len = seq // dil
        blocks_per_res = m_len // QBLK

        def group_body(it, carry, dil=dil, m_len=m_len, blocks_per_res=blocks_per_res,
                       first=(branch == 0)):
            q_rows, qs, ks, vs, in_range = [], [], [], [], []
            for g in range(ATTN_GROUP):
                blk = it * ATTN_GROUP + g
                res = blk // blocks_per_res
                m0 = (blk % blocks_per_res) * QBLK
                q_start = res + dil * m0
                k_start = pad + res + dil * (m0 - half_w)
                if dil == 1:
                    rows = pl.ds(pl.multiple_of(q_start, QBLK), QBLK)
                    k_rows = pl.ds(pl.multiple_of(k_start, half_w), KBLK)
                else:
                    rows = pl.ds(q_start, QBLK, stride=dil)
                    k_rows = pl.ds(k_start, KBLK, stride=dil)
                q_rows.append(rows)
                if dil == big_dil:
                    g_rows = pl.ds(pl.multiple_of(res * pitch + m0, half_w), KBLK)
                    qb = qg[pl.ds(pl.multiple_of(res * seg + m0, QBLK), QBLK), :]
                    kb, vb = kg[g_rows, :], vg[g_rows, :]
                else:
                    qb, kb, vb = qr[rows, :], kp[k_rows, :], vp[k_rows, :]
                qs.append(qb.astype(BF16))
                ks.append(stack_rows(kb))
                vs.append(jnp.concatenate([stack_rows(vb), ones_st], axis=1))
                kpos = m0 - half_w + kj
                in_range.append((kpos >= 0) & (kpos < m_len))
            sc = _bmm(jnp.stack(qs), jnp.stack(ks), _BNT)
            sc = jnp.stack([jnp.where(in_range[g], jnp.where(band, sc[g], NEG_BIG), NEG_BIG)
                            for g in range(ATTN_GROUP)])
            tile_max = lambda x: jnp.maximum(x[:, :, :KBLK // 2], x[:, :, KBLK // 2:])
            mx0 = jnp.max(tile_max(sc[:, :, :KBLK]), axis=-1, keepdims=True)
            mx1 = jnp.max(tile_max(sc[:, :, KBLK:]), axis=-1, keepdims=True)
            p = jnp.exp(sc - jnp.where(left, mx0, mx1)).astype(BF16)
            nd = lax.dot_general(p, jnp.stack(vs), _BNN, preferred_element_type=F32)
            n_b = nd[:, :, :PAIR]
            d_b = nd[:, :, PAIR:]
            m_b = jnp.where(head0, mx0, mx1)
            for g, rows in enumerate(q_rows):
                if first:
                    m_s[rows, :] = m_b[g]
                    den_s[rows, :] = d_b[g]
                    num_s[rows, :] = n_b[g]
                else:
                    m_old = m_s[rows, :]
                    m_new = jnp.maximum(m_old, m_b[g])
                    alpha = jnp.exp(m_old - m_new)
                    beta = jnp.exp(m_b[g] - m_new)
                    m_s[rows, :] = m_new
                    den_s[rows, :] = alpha * den_s[rows, :] + beta * d_b[g]
                    num_s[rows, :] = alpha * num_s[rows, :] + beta * n_b[g]
            return carry

        lax.fori_loop(0, seq // (QBLK * ATTN_GROUP), group_body, 0)

    def out_body(j, carry):
        rows = pl.ds(pl.multiple_of(j * rope_rows, rope_rows), rope_rows)
        o_ref[0, rows, :] = (num_s[rows, :] / den_s[rows, :]).astype(o_ref.dtype)
        return carry

    lax.fori_loop(0, seq // rope_rows, out_body, 0)


def _dilated_attention(z, col0, width):
    bsz, s, _ = z.shape
    n_pairs = width // PAIR
    max_dil = max(d for _, d in DILATED_PAIRS)
    assert s % (QBLK * max_dil) == 0 and s % (QBLK * ATTN_GROUP) == 0 and col0 % PAIR == 0
    pad = (QBLK // 2) * max_dil
    half = HEAD_DIM // 2
    inv = ROPE_THETA ** (-jnp.arange(half, dtype=F32) / half)
    ang = jnp.arange(s, dtype=F32)[:, None] * inv[None, :]
    cos = jnp.tile(jnp.cos(ang), (1, PAIR // half))
    sin = jnp.tile(jnp.concatenate([-jnp.sin(ang), jnp.sin(ang)], axis=1), (1, 2))
    per = QBLK // max_dil
    row = np.arange(QBLK)
    perm = jnp.asarray(row[None, :] == (row[:, None] % per) * max_dil + row[:, None] // per, BF16)
    lane = np.arange(PAIR)
    partner_lane = np.where(lane % HEAD_DIM < half, lane + half, lane - half)
    swap = jnp.asarray(lane[:, None] == partner_lane[None, :], BF16)
    assert z.dtype == BF16
    cb = col0 // PAIR
    col = lambda off: pl.BlockSpec((1, s, PAIR), lambda b, p: (b, 0, cb + off * n_pairs + p))
    rope_rows = 512
    return pl.pallas_call(
        functools.partial(_dilated_kernel, seq=s, pad=pad, rope_rows=rope_rows, big_dil=max_dil),
        grid=(bsz, n_pairs),
        in_specs=[col(0), col(1), col(2), _resident((s, PAIR)), _resident((s, PAIR)),
                  _resident((PAIR, PAIR)), _resident((QBLK, QBLK))],
        out_specs=pl.BlockSpec((1, s, PAIR), lambda b, p: (b, 0, p)),
        out_shape=jax.ShapeDtypeStruct((bsz, s, width), BF16),
        scratch_shapes=[pltpu.VMEM((s, PAIR), F32),
                        pltpu.VMEM((s + 2 * pad, PAIR), F32),
                        pltpu.VMEM((s + 2 * pad, PAIR), F32),
                        pltpu.VMEM((s, PAIR), F32),
                        pltpu.VMEM((s + QBLK * max_dil, PAIR), F32),
                        pltpu.VMEM((s + QBLK * max_dil, PAIR), F32),
                        pltpu.VMEM((s, PAIR), F32),
                        pltpu.VMEM((s, PAIR), F32),
                        pltpu.VMEM((s, PAIR), F32)],
        compiler_params=_cparams(("parallel", "parallel")),
        name="dilated_attention",
    )(z, z, z, cos, sin, swap, perm)


def _seq_dft_kernel(xc_ref, xs_ref, cm_ref, sm_ref, jm_ref, alt_ref, o_ref, xe_s, yo_s,
                    *, seq, t):
    h = seq // 2
    nb = h // t
    scale = seq ** -0.5
    jm = jm_ref[...]
    row0 = lax.broadcasted_iota(jnp.int32, (t, 1), 0) == 0
    for i in range(nb):
        rows = slice(i * t, (i + 1) * t)
        mirror = slice(seq - (i + 1) * t, seq - i * t)
        pc = jnp.dot(jm, xc_ref[0, mirror, :], preferred_element_type=F32)
        ps = jnp.dot(jm, xs_ref[0, mirror, :], preferred_element_type=F32)
        if i > 0:
            first = slice(seq - i * t, seq - i * t + 1)
            pc = jnp.where(row0, xc_ref[0, first, :].astype(F32), pc)
            ps = jnp.where(row0, xs_ref[0, first, :].astype(F32), ps)
        xe_s[rows, :] = (xc_ref[0, rows, :].astype(F32) + pc).astype(BF16)
        yo_s[rows, :] = (xs_ref[0, rows, :].astype(F32) - ps).astype(BF16)
    xe = xe_s[...]
    p = jnp.dot(cm_ref[...], xe, preferred_element_type=F32)
    q = jnp.dot(sm_ref[...], yo_s[...], preferred_element_type=F32)
    alt = alt_ref[...]
    x_h = xc_ref[0, h:h + 1, :].astype(F32) * scale
    k_odd = lax.broadcasted_iota(jnp.int32, (h, 1), 0) % 2 == 1
    alt_x = jnp.where(k_odd, -x_h, x_h)
    o_ref[0, 0:h, :] = (p - q + alt_x).astype(o_ref.dtype)
    pq = (p + q + alt_x).astype(BF16)
    f_h = scale * jnp.dot(alt, xe, preferred_element_type=F32)[0:1] + x_h
    for i in range(nb):
        rev = jnp.dot(jm, pq[h - (i + 1) * t:h - i * t], preferred_element_type=F32)
        first = f_h if i == 0 else pq[h - i * t:h - i * t + 1].astype(F32)
        o_ref[0, h + i * t:h + (i + 1) * t, :] = jnp.where(row0, first, rev).astype(o_ref.dtype)


def _seq_dft(cm, sm, x2, tn=256, t=128):
    bsz, s, two_d = x2.shape
    d = two_d // 2
    h = s // 2
    assert h % 2 == 0 and h % t == 0 and d % tn == 0
    r = np.arange(t)
    jm = jnp.asarray((r[None, :] == t - r[:, None]) & (r[:, None] >= 1), BF16)
    alt = jnp.asarray(np.where(np.arange(8)[:, None] == 0, 1.0 - 2.0 * (np.arange(h) % 2), 0.0), BF16)
    nj = d // tn
    return pl.pallas_call(
        functools.partial(_seq_dft_kernel, seq=s, t=t),
        grid=(bsz, nj),
        in_specs=[pl.BlockSpec((1, s, tn), lambda b, j: (b, 0, j)),
                  pl.BlockSpec((1, s, tn), lambda b, j: (b, 0, nj + j)),
                  _resident((h, h)), _resident((h, h)), _resident((t, t)), _resident((8, h))],
        out_specs=pl.BlockSpec((1, s, tn), lambda b, j: (b, 0, j)),
        out_shape=jax.ShapeDtypeStruct((bsz, s, d), BF16),
        scratch_shapes=[pltpu.VMEM((h, tn), BF16), pltpu.VMEM((h, tn), BF16)],
        compiler_params=_cparams(("parallel", "parallel")),
        name="seq_dft",
    )(x2, x2, cm, sm, jm, alt)


def _dft_tables(s, d):
    gsz = d // N_FOURIER_GROUPS
    cidx = jnp.arange(gsz, dtype=jnp.int32)
    ang_c = (2.0 * np.pi / gsz) * ((cidx[:, None] * cidx[None, :]) % gsz).astype(F32)
    w_ch = jnp.concatenate([jnp.cos(ang_c), jnp.sin(ang_c)], axis=1) * (gsz ** -0.5)
    q = 64
    h = s // 2
    assert h % q == 0
    sidx = jnp.arange(h, dtype=jnp.int32)[None, :]
    ang = lambda kvals: (2.0 * np.pi / s) * ((kvals[:, None] * sidx) % s).astype(F32)
    ang_hi = ang(q * jnp.arange(h // q, dtype=jnp.int32))[:, None, :]
    ang_lo = ang(jnp.arange(q, dtype=jnp.int32))[None, :, :]
    scale = s ** -0.5
    ch, sh, cl, sl = jnp.cos(ang_hi) * scale, jnp.sin(ang_hi) * scale, jnp.cos(ang_lo), jnp.sin(ang_lo)
    cm = (ch * cl - sh * sl).reshape(h, h).astype(BF16)
    sm = (sh * cl + ch * sl).reshape(h, h).astype(BF16)
    return w_ch.astype(BF16), cm, sm


def kernel(x, c, ada_w, ada_b, norm_g, mix_in, mix_conv, rwkv_w0, rwkv_w1, rwkv_w2, rwkv_a0, rwkv_a1, rwkv_a2, rwkv_g1, rwkv_g2, rwkv_k_k, rwkv_k_a, rwkv_r_k, rwkv_lnx_w, rwkv_lnx_b, mix_out, fnet_w, ffn_up, ffn_conv, ffn_down):
    bsz, s, d = x.shape
    depth = ada_w.shape[0]
    width_a = rwkv_k_k.shape[-1]
    width_b = mix_out.shape[1] - width_a
    in_cols_a = 4 * width_a
    ones_bd = jnp.kron(jnp.eye(width_a // HEAD_DIM, dtype=F32),
                       jnp.ones((HEAD_DIM, HEAD_DIM), F32)).astype(BF16)

    mod = _adaln_mod(c, ada_w, ada_b)
    for l in range(depth):
        sh1, sc1, gt1, sh2, sc2, gt2 = [mod[l, :, j * d:(j + 1) * d] for j in range(6)]
        if l % 2 == 0:
            e = l // 2
            zb, r, v, kk, g, bonus, kd, bb, lw = _mix_prep(
                x, norm_g[l, 0], sc1, sh1, mix_in[e], mix_conv[e], rwkv_w0[e], rwkv_w1[e],
                rwkv_w2[e], rwkv_a0[e], rwkv_a1[e], rwkv_a2[e], rwkv_g1[e], rwkv_g2[e],
                rwkv_k_k[e], rwkv_k_a[e], rwkv_r_k[e], ones_bd)
            y0, y1 = _wkv_scan(r, v, kk, kd, bb, lw)
            yb = _dilated_attention(zb, 0, width_b)
            x = _mix_out(y0, y1, g, bonus, rwkv_lnx_w[e], rwkv_lnx_b[e], ones_bd, yb,
                         mix_out[e], norm_g[l, 1], gt1, x)
        else:
            w_ch, cm, sm = _dft_tables(s, d)
            x2 = _chan_dft(x, norm_g[l, 0], sc1, sh1, w_ch)
            f = _seq_dft(cm, sm, x2)
            x = _out_proj([f], [fnet_w[l // 2]], norm_g[l, 1], gt1, x)
        x = _ffn(x, norm_g[l, 2], sc2, sh2, ffn_up[l], ffn_conv[l], ffn_down[l],
                 norm_g[l, 3], gt2)
    return x
```
